```python
import jax, jax.numpy as jnp
from jax import lax
import numpy as np

D_MODEL = 1024
BATCH = 16
SEQ = 2048
DEPTH = 4

N_MIXERS = 2
N_A = (DEPTH + 1) // 2
N_B = DEPTH // 2
EPS = 1e-6

M_HEADS = 8
M_DV = D_MODEL // M_HEADS
M_DK = M_DV // 2
M_CHUNK = 128
GATE_CAP = 15.0
A_IN = 2 * M_HEADS * M_DK + 2 * D_MODEL + 2 * M_HEADS

G_CHUNK = 128
G_FFN = 4 * D_MODEL
G_HALF = G_FFN // 2
G_GROUP = 128
G_GROUPS = G_HALF // G_GROUP

FFN_HIDDEN = 4 * D_MODEL

kernel_name = "hybrid_mlstm_sgu_adaln_trunk"


def rmsnorm(x, g):
    xf = x.astype(jnp.float32)
    y = xf * lax.rsqrt(jnp.mean(xf * xf, axis=-1, keepdims=True) + EPS) * g.astype(jnp.float32)
    return y.astype(x.dtype)


def layernorm(x, g, b):
    xf = x.astype(jnp.float32)
    mu = jnp.mean(xf, axis=-1, keepdims=True)
    var = jnp.mean(jnp.square(xf - mu), axis=-1, keepdims=True)
    y = (xf - mu) * lax.rsqrt(var + EPS) * g.astype(jnp.float32) + b.astype(jnp.float32)
    return y.astype(x.dtype)


def _to_chunks(t, nc, d):
    bsz = t.shape[0]
    return t.reshape(bsz, nc, M_CHUNK, M_HEADS, d).transpose(0, 3, 1, 2, 4).astype(jnp.float32)


def mlstm_mixer(h, w_in, b_if, hnorm_g, w_out):
    bsz, s, _ = h.shape
    nc = s // M_CHUNK
    hk = M_HEADS * M_DK
    p = h @ w_in
    q, k, v, o, ig, fg = jnp.split(
        p, [hk, 2 * hk, 2 * hk + D_MODEL, 2 * hk + 2 * D_MODEL, 2 * hk + 2 * D_MODEL + M_HEADS], axis=-1)
    q = _to_chunks(q, nc, M_DK)
    k = _to_chunks(k, nc, M_DK) * (M_DK ** -0.5)
    v = _to_chunks(v, nc, M_DV)
    b_if = b_if.astype(jnp.float32)
    ig = ig.astype(jnp.float32) + b_if[:M_HEADS]
    fg = fg.astype(jnp.float32) + b_if[M_HEADS:]
    ig = GATE_CAP * jnp.tanh(ig / GATE_CAP)
    fg = GATE_CAP * jnp.tanh(fg / GATE_CAP)
    ii = ig.reshape(bsz, nc, M_CHUNK, M_HEADS).transpose(0, 3, 1, 2)
    logf = jax.nn.log_sigmoid(fg).reshape(bsz, nc, M_CHUNK, M_HEADS).transpose(0, 3, 1, 2)

    bcum = jnp.cumsum(logf, axis=-1)
    b_last = bcum[..., -1]

    a = b_last[..., None] - bcum + ii
    m_loc = jnp.max(a, axis=-1)
    wgt = jnp.exp(a - m_loc[..., None])
    c_loc = jnp.einsum('bhcl,bhcld,bhcle->bhcde', wgt, k, v)
    n_loc = jnp.einsum('bhcl,bhcld->bhcd', wgt, k)

    def step(carry, inp):
        c_st, n_st, m_st = carry
        cl, nl, ml, bl = inp
        m_new = jnp.maximum(bl + m_st, ml)
        s_old = jnp.exp(bl + m_st - m_new)
        s_new = jnp.exp(ml - m_new)
        c_new = s_old[..., None, None] * c_st + s_new[..., None, None] * cl
        n_new = s_old[..., None] * n_st + s_new[..., None] * nl
        return (c_new, n_new, m_new), (c_st, n_st, m_st)

    init = (jnp.zeros((bsz, M_HEADS, M_DK, M_DV), jnp.float32),
            jnp.zeros((bsz, M_HEADS, M_DK), jnp.float32),
            jnp.zeros((bsz, M_HEADS), jnp.float32))
    xs = (jnp.moveaxis(c_loc, 2, 0), jnp.moveaxis(n_loc, 2, 0),
          jnp.moveaxis(m_loc, 2, 0), jnp.moveaxis(b_last, 2, 0))
    _, (c_prev, n_prev, m_prev) = lax.scan(step, init, xs)
    c_prev = jnp.moveaxis(c_prev, 0, 2)
    n_prev = jnp.moveaxis(n_prev, 0, 2)
    m_prev = jnp.moveaxis(m_prev, 0, 2)

    causal = jnp.tril(jnp.ones((M_CHUNK, M_CHUNK), dtype=bool))
    dmat = bcum[..., :, None] - bcum[..., None, :] + ii[..., None, :]
    dmat = jnp.where(causal, dmat, -jnp.inf)
    g_inter = bcum + m_prev[..., None]
    m_j = jnp.maximum(g_inter, jnp.max(dmat, axis=-1))
    decay = jnp.exp(dmat - m_j[..., None])
    scores = jnp.einsum('bhcid,bhcjd->bhcij', q, k) * decay
    inter = jnp.exp(g_inter - m_j)
    num = (inter[..., None] * jnp.einsum('bhcid,bhcde->bhcie', q, c_prev)
           + jnp.einsum('bhcij,bhcje->bhcie', scores, v))
    den = inter * jnp.einsum('bhcid,bhcd->bhci', q, n_prev) + jnp.sum(scores, axis=-1)
    hh = num / jnp.maximum(jnp.abs(den), jnp.exp(-m_j))[..., None]

    hh = hh * lax.rsqrt(jnp.mean(hh * hh, axis=-1, keepdims=True) + EPS)
    hh = hh.transpose(0, 2, 3, 1, 4).reshape(bsz, s, M_HEADS * M_DV)
    hh = hh * hnorm_g.astype(jnp.float32)
    out = jax.nn.sigmoid(o.astype(jnp.float32)) * hh
    return out.astype(h.dtype) @ w_out


def sgu_mixer(h, w_in, b_in, ln_g, ln_b, ws, bs, w_out):
    bsz, s, _ = h.shape
    nc = s // G_CHUNK
    z = jax.nn.gelu(h @ w_in + b_in, approximate=False)
    u, v = jnp.split(z, 2, axis=-1)
    v = layernorm(v, ln_g, ln_b)
    vr = v.reshape(bsz, nc, G_CHUNK, G_GROUPS, G_GROUP)
    wm = ws * jnp.tril(jnp.ones((G_CHUNK, G_CHUNK), ws.dtype))
    sv = jnp.einsum('gts,bnsgc->bntgc', wm, vr) + bs.T[:, :, None]
    y = u * sv.reshape(bsz, s, G_HALF)
    return y @ w_out


def setup_inputs(seed: int = 0) -> dict:
    key = jax.random.key(seed)
    ks = jax.random.split(key, 20)
    f32 = jnp.float32
    nrm = lambda k, shape, scale: jax.random.normal(k, shape, f32) * scale
    return {
        "x": nrm(ks[0], (BATCH, SEQ, D_MODEL), 1.0),
        "c": nrm(ks[1], (BATCH, D_MODEL), 1.0),
        "norm_g": 1.0 + nrm(ks[2], (DEPTH, 4, D_MODEL), 0.1),
        "ada_w": nrm(ks[3], (DEPTH, D_MODEL, 6 * D_MODEL), 0.5 * D_MODEL ** -0.5),
        "ada_b": nrm(ks[4], (DEPTH, 6 * D_MODEL), 0.02),
        "ffn_w1": nrm(ks[5], (DEPTH, D_MODEL, FFN_HIDDEN), D_MODEL ** -0.5),
        "ffn_w2": nrm(ks[6], (DEPTH, FFN_HIDDEN, D_MODEL), FFN_HIDDEN ** -0.5),
        "a_w_in": nrm(ks[7], (N_A, D_MODEL, A_IN), D_MODEL ** -0.5),
        "a_b_if": jnp.concatenate([nrm(ks[8], (N_A, M_HEADS), 0.1),
                                   3.0 + nrm(ks[9], (N_A, M_HEADS), 0.5)], axis=-1),
        "a_hnorm_g": 1.0 + nrm(ks[10], (N_A, M_HEADS * M_DV), 0.1),
        "a_w_out": nrm(ks[11], (N_A, M_HEADS * M_DV, D_MODEL), (M_HEADS * M_DV) ** -0.5),
        "b_w_in": nrm(ks[12], (N_B, D_MODEL, G_FFN), D_MODEL ** -0.5),
        "b_b_in": nrm(ks[13], (N_B, G_FFN), 0.02),
        "b_ln_g": 1.0 + nrm(ks[14], (N_B, G_HALF), 0.1),
        "b_ln_b": nrm(ks[15], (N_B, G_HALF), 0.02),
        "b_ws": nrm(ks[16], (N_B, G_GROUPS, G_CHUNK, G_CHUNK), G_CHUNK ** -0.5),
        "b_bs": 1.0 + nrm(ks[17], (N_B, G_GROUPS, G_CHUNK), 0.1),
        "b_w_out": nrm(ks[18], (N_B, G_HALF, D_MODEL), G_HALF ** -0.5),
    }


def reference(x, c, norm_g, ada_w, ada_b, ffn_w1, ffn_w2,
              a_w_in, a_b_if, a_hnorm_g, a_w_out,
              b_w_in, b_b_in, b_ln_g, b_ln_b, b_ws, b_bs, b_w_out):
    c_act = jax.nn.silu(c)
    for i in range(DEPTH):
        mod = c_act @ ada_w[i] + ada_b[i]
        sh1, sc1, g1, sh2, sc2, g2 = jnp.split(mod[:, None, :], 6, axis=-1)
        h = rmsnorm(x, norm_g[i, 0]) * (1.0 + sc1) + sh1
        j = i // N_MIXERS
        if i % N_MIXERS == 0:
            y = mlstm_mixer(h, a_w_in[j], a_b_if[j], a_hnorm_g[j], a_w_out[j])
        else:
            y = sgu_mixer(h, b_w_in[j], b_b_in[j], b_ln_g[j], b_ln_b[j], b_ws[j], b_bs[j], b_w_out[j])
        x = x + g1 * rmsnorm(y, norm_g[i, 1])
        h = rmsnorm(x, norm_g[i, 2]) * (1.0 + sc2) + sh2
        y = jnp.square(jax.nn.relu(h @ ffn_w1[i])) @ ffn_w2[i]
        x = x + g2 * rmsnorm(y, norm_g[i, 3])
    return x
```

```python
import functools

import jax
import jax.numpy as jnp
from jax import lax
from jax.experimental import pallas as pl
from jax.experimental.pallas import tpu as pltpu

F32 = jnp.float32
BF16 = jnp.bfloat16

D_MODEL = 1024
DEPTH = 4
EPS = 1e-6

M_HEADS = 8
M_DV = 128
M_DK = 64
M_CHUNK = 128
GATE_CAP = 15.0
M_QK = M_HEADS * M_DK
M_QKVO = 2 * M_QK + 2 * D_MODEL

G_CHUNK = 128
G_HALF = 2048
G_GROUP = 128
G_GROUPS = 16

FFN_HIDDEN = 4096

LANES = 128
VMEM_LIMIT_BYTES = 56 * 1024 * 1024

TOKEN_TILE = 512
FFN_HIDDEN_TILE = 1024


def _dot(a, b):
    return jnp.dot(a, b, preferred_element_type=F32)


def _rms(x, g):
    return x * lax.rsqrt(jnp.mean(x * x, axis=-1, keepdims=True) + EPS) * g


def _resident(shape):
    zeros = (0,) * len(shape)
    return pl.BlockSpec(shape, lambda *_: zeros, pipeline_mode=pl.Buffered(1))


def _params(semantics):
    return pltpu.CompilerParams(dimension_semantics=semantics,
                                vmem_limit_bytes=VMEM_LIMIT_BYTES)


def _ada_kernel(c_ref, w_ref, b_ref, o_ref):
    c = c_ref[...]
    ca = (c * jax.nn.sigmoid(c)).astype(BF16)
    o_ref[...] = _dot(ca, w_ref[...].astype(BF16)) + b_ref[...]


def _ada(c, ada_w, ada_b):
    bsz = c.shape[0]
    blk = D_MODEL
    return pl.pallas_call(
        _ada_kernel,
        grid=(DEPTH, 6 * D_MODEL // blk),
        in_specs=[
            pl.BlockSpec((bsz, D_MODEL), lambda i, j: (0, 0)),
            pl.BlockSpec((None, D_MODEL, blk), lambda i, j: (i, 0, j)),
            pl.BlockSpec((None, 1, blk), lambda i, j: (i, 0, j)),
        ],
        out_specs=pl.BlockSpec((None, bsz, blk), lambda i, j: (i, 0, j)),
        out_shape=jax.ShapeDtypeStruct((DEPTH, bsz, 6 * D_MODEL), F32),
        compiler_params=_params(("arbitrary", "arbitrary")),
        name="ada_mod",
    )(c, ada_w, ada_b.reshape(DEPTH, 1, 6 * D_MODEL))


def _ffn_kernel(x_ref, mod_ref, ng_ref, w1_ref, w2_ref, o_ref):
    x = x_ref[...]
    mod = mod_ref[...]
    ng = ng_ref[...]
    h = (_rms(x, ng[2:3]) * (1.0 + mod[4:5]) + mod[3:4]).astype(BF16)
    y = jnp.zeros(x.shape, F32)
    for k0 in range(0, FFN_HIDDEN, FFN_HIDDEN_TILE):
        a = _dot(h, w1_ref[:, k0:k0 + FFN_HIDDEN_TILE])
        a = jnp.square(jnp.maximum(a, 0.0)).astype(BF16)
        y = y + _dot(a, w2_ref[k0:k0 + FFN_HIDDEN_TILE, :])
    o_ref[...] = x + mod[5:6] * _rms(y, ng[3:4])


def _ffn(x2, mod, ng, w1, w2, seq):
    tokens = x2.shape[0]
    tm = TOKEN_TILE
    per_seq = seq // tm
    return pl.pallas_call(
        _ffn_kernel,
        grid=(tokens // tm,),
        in_specs=[
            pl.BlockSpec((tm, D_MODEL), lambda i: (i, 0)),
            pl.BlockSpec((None, 6, D_MODEL), lambda i: (i // per_seq, 0, 0)),
            _resident((4, D_MODEL)),
            _resident((D_MODEL, FFN_HIDDEN)),
            _resident((FFN_HIDDEN, D_MODEL)),
        ],
        out_specs=pl.BlockSpec((tm, D_MODEL), lambda i: (i, 0)),
        out_shape=jax.ShapeDtypeStruct(x2.shape, F32),
        compiler_params=_params(("arbitrary",)),
        name="ffn_relu2",
    )(x2, mod, ng, w1, w2)


def _gelu(z):
    return 0.5 * z * (1.0 + lax.erf(z * (2.0 ** -0.5)))


def _sgu_kernel(x_ref, mod_ref, ng_ref, win_ref, bin_ref, lng_ref, lnb_ref,
                ws_ref, bsb_ref, wout_ref, o_ref, y_scr):
    x = x_ref[...]
    mod = mod_ref[...]
    ng = ng_ref[...]
    tm = x.shape[0]
    h = (_rms(x, ng[0:1]) * (1.0 + mod[1:2]) + mod[0:1]).astype(BF16)
    zu = _gelu(_dot(h, win_ref[:, :G_HALF]) + bin_ref[:, :G_HALF])
    zv = _gelu(_dot(h, win_ref[:, G_HALF:]) + bin_ref[:, G_HALF:])
    mu = jnp.mean(zv, axis=-1, keepdims=True)
    zc = zv - mu
    var = jnp.mean(zc * zc, axis=-1, keepdims=True)
    vn = (zc * lax.rsqrt(var + EPS) * lng_ref[...] + lnb_ref[...]).astype(BF16)

    row = lax.broadcasted_iota(jnp.int32, (G_CHUNK, G_CHUNK), 0)
    col = lax.broadcasted_iota(jnp.int32, (G_CHUNK, G_CHUNK), 1)
    causal = col <= row
    for g in range(G_GROUPS):
        c0 = g * G_GROUP
        wm = jnp.where(causal, ws_ref[g], 0.0).astype(BF16)
        bsg = bsb_ref[g]
        for r0 in range(0, tm, 2 * G_CHUNK):
            r1 = r0 + G_CHUNK
            r2 = r1 + G_CHUNK
            rhs = jnp.concatenate([vn[r0:r1, c0:c0 + G_GROUP],
                                   vn[r1:r2, c0:c0 + G_GROUP]], axis=1)
            sv = _dot(wm, rhs)
            y_scr[r0:r1, c0:c0 + G_GROUP] = (
                zu[r0:r1, c0:c0 + G_GROUP] * (sv[:, :G_GROUP] + bsg)).astype(BF16)
            y_scr[r1:r2, c0:c0 + G_GROUP] = (
                zu[r1:r2, c0:c0 + G_GROUP] * (sv[:, G_GROUP:] + bsg)).astype(BF16)
    y = _dot(y_scr[...], wout_ref[...])
    o_ref[...] = x + mod[2:3] * _rms(y, ng[1:2])


def _sgu(x2, mod, ng, w_in, b_in, ln_g, ln_b, ws, bs, w_out, seq):
    tokens = x2.shape[0]
    tm = TOKEN_TILE
    per_seq = seq // tm
    bsb = jnp.broadcast_to(bs[:, :, None], (G_GROUPS, G_CHUNK, LANES))
    return pl.pallas_call(
        _sgu_kernel,
        grid=(tokens // tm,),
        in_specs=[
            pl.BlockSpec((tm, D_MODEL), lambda i: (i, 0)),
            pl.BlockSpec((None, 6, D_MODEL), lambda i: (i // per_seq, 0, 0)),
            _resident((4, D_MODEL)),
            _resident((D_MODEL, 2 * G_HALF)),
            _resident((1, 2 * G_HALF)),
            _resident((1, G_HALF)),
            _resident((1, G_HALF)),
            _resident((G_GROUPS, G_CHUNK, G_CHUNK)),
            _resident((G_GROUPS, G_CHUNK, LANES)),
            _resident((G_HALF, D_MODEL)),
        ],
        out_specs=pl.BlockSpec((tm, D_MODEL), lambda i: (i, 0)),
        out_shape=jax.ShapeDtypeStruct(x2.shape, F32),
        scratch_shapes=[pltpu.VMEM((tm, G_HALF), BF16)],
        compiler_params=_params(("arbitrary",)),
        name="sgu_mixer",
    )(x2, mod, ng, w_in, b_in.reshape(1, -1), ln_g.reshape(1, -1),
      ln_b.reshape(1, -1), ws, bsb, w_out)


def _split3(v):
    hi = v.astype(BF16)
    r1 = v - hi.astype(F32)
    mid = r1.astype(BF16)
    lo = (r1 - mid.astype(F32)).astype(BF16)
    return hi, mid, lo


def _mlstm_kernel(x_ref, mod_ref, ng_ref, wqkvo_ref, wif_ref, bif_ref, hg_ref,
                  wout_ref, o_ref, p_scr, g_scr, y_scr, s_scr, m_scr):
    L = M_CHUNK

    @pl.when(pl.program_id(1) == 0)
    def _():
        s_scr[...] = jnp.zeros(s_scr.shape, F32)
        m_scr[...] = jnp.zeros(m_scr.shape, F32)

    x = x_ref[...]
    mod = mod_ref[...]
    ng = ng_ref[...]
    ts = x.shape[0]
    h = (_rms(x, ng[0:1]) * (1.0 + mod[1:2]) + mod[0:1]).astype(BF16)
    p_scr[...] = _dot(h, wqkvo_ref[...])
    g_scr[...] = _dot(h, wif_ref[...]) + bif_ref[...]

    row = lax.broadcasted_iota(jnp.int32, (L, L), 0)
    col = lax.broadcasted_iota(jnp.int32, (L, L), 1)
    causal = col <= row
    tri = jnp.where(causal, 1.0, 0.0).astype(BF16)
    ones_blk = jnp.ones((L, M_DV), BF16)
    zeros_k = jnp.zeros((M_DK, L), F32)
    zeros_s = jnp.zeros((M_DK, 2 * M_DV), F32)

    def chunk(c, carry):
        r0 = pl.multiple_of(c * L, L)
        rows = pl.ds(r0, L)
        gcap = GATE_CAP * jnp.tanh(g_scr[rows, :] / GATE_CAP)
        logf = jax.nn.log_sigmoid(gcap)
        hi, mid, lo = _split3(logf)
        bcum = _dot(tri, hi) + _dot(tri, mid) + _dot(tri, lo)
        ii_row = gcap.T[0:M_HEADS, :]
        bc_row = bcum.T[M_HEADS:2 * M_HEADS, :]
        r_row = ii_row - bc_row
        rmax = jnp.max(r_row, axis=1, keepdims=True)
        wgt_row = jnp.exp(r_row - rmax)
        b_last = bc_row[:, L - 1:L]
        m_st = m_scr[:, 0:1]
        m_loc = b_last + rmax
        m_new = jnp.maximum(b_last + m_st, m_loc)
        s_old = jnp.exp(b_last + m_st - m_new)
        s_new = jnp.exp(m_loc - m_new)
        m_scr[...] = jnp.broadcast_to(m_new, m_scr.shape)

        for pair in range(M_HEADS // 2):
            pc = pair * LANES
            q2 = p_scr[rows, pc:pc + LANES].astype(BF16)
            kT = (p_scr[rows, M_QK + pc:M_QK + pc + LANES] * (M_DK ** -0.5)).T
            s_pair = s_scr[pair]
            for half in range(2):
                hd = 2 * pair + half
                vc = 2 * M_QK + hd * M_DV
                oc = vc + D_MODEL
                kT_h = kT[half * M_DK:(half + 1) * M_DK, :]
                s_h = s_pair[half * M_DK:(half + 1) * M_DK, :]
                if half == 0:
                    kTz = jnp.concatenate([kT_h, zeros_k], axis=0)
                    sz = jnp.concatenate([s_h, zeros_s], axis=0)
                else:
                    kTz = jnp.concatenate([zeros_k, kT_h], axis=0)
                    sz = jnp.concatenate([zeros_s, s_h], axis=0)
                vext = jnp.concatenate([p_scr[rows, vc:vc + M_DV].astype(BF16), ones_blk], axis=1)

                rmat = jnp.where(causal, r_row[hd:hd + 1, :], -jnp.inf)
                mp = m_st[hd:hd + 1, :]
                mx = jnp.maximum(jnp.max(rmat, axis=1, keepdims=True), mp)
                decay = jnp.exp(rmat - mx)
                inter = jnp.exp(mp - mx)
                floor = jnp.exp(-(bcum[:, M_HEADS + hd:M_HEADS + hd + 1] + mx))

                scores = (_dot(q2, kTz.astype(BF16)) * decay).astype(BF16)
                intra = _dot(scores, vext)
                cross = _dot(q2, sz.astype(BF16))
                num = inter * cross[:, :M_DV] + intra[:, :M_DV]
                den = inter * cross[:, M_DV:] + intra[:, M_DV:]
                hh = num / jnp.maximum(jnp.abs(den), floor)
                hh = hh * lax.rsqrt(jnp.mean(hh * hh, axis=-1, keepdims=True) + EPS)
                hh = hh * hg_ref[:, hd * M_DV:(hd + 1) * M_DV]
                out = jax.nn.sigmoid(p_scr[rows, oc:oc + M_DV]) * hh
                y_scr[rows, hd * M_DV:(hd + 1) * M_DV] = out.astype(BF16)

                wk = (kT_h * wgt_row[hd:hd + 1, :]).astype(BF16)
                upd = _dot(wk, vext)
                s_scr[pair, half * M_DK:(half + 1) * M_DK, :] = (
                    s_old[hd:hd + 1, :] * s_h + s_new[hd:hd + 1, :] * upd)
        return carry

    lax.fori_loop(0, ts // L, chunk, 0)
    y = _dot(y_scr[...], wout_ref[...])
    o_ref[...] = x + mod[2:3] * _rms(y, ng[1:2])


def _mlstm(x2, mod, ng, w_in, b_if, hnorm_g, w_out, bsz, seq):
    ts = TOKEN_TILE
    per_seq = seq // ts
    w_qkvo = w_in[:, :M_QKVO].astype(BF16)
    w_if = jnp.pad(w_in[:, M_QKVO:], ((0, 0), (0, LANES - 2 * M_HEADS))).astype(BF16)
    b_pad = jnp.pad(b_if, (0, LANES - 2 * M_HEADS)).reshape(1, LANES)
    return pl.pallas_call(
        _mlstm_kernel,
        grid=(bsz, per_seq),
        in_specs=[
            pl.BlockSpec((ts, D_MODEL), lambda b, t: (b * per_seq + t, 0)),
            pl.BlockSpec((None, 6, D_MODEL), lambda b, t: (b, 0, 0)),
            _resident((4, D_MODEL)),
            _resident((D_MODEL, M_QKVO)),
            _resident((D_MODEL, LANES)),
            _resident((1, LANES)),
            _resident((1, D_MODEL)),
            _resident((D_MODEL, D_MODEL)),
        ],
        out_specs=pl.BlockSpec((ts, D_MODEL), lambda b, t: (b * per_seq + t, 0)),
        out_shape=jax.ShapeDtypeStruct(x2.shape, F32),
        scratch_shapes=[
            pltpu.VMEM((ts, M_QKVO), F32),
            pltpu.VMEM((ts, LANES), F32),
            pltpu.VMEM((ts, D_MODEL), BF16),
            pltpu.VMEM((M_HEADS // 2, 2 * M_DK, 2 * M_DV), F32),
            pltpu.VMEM((M_HEADS, LANES), F32),
        ],
        compiler_params=_params(("arbitrary", "arbitrary")),
        name="mlstm_mixer",
    )(x2, mod, ng, w_qkvo, w_if, b_pad, hnorm_g.reshape(1, -1), w_out.astype(BF16))


def kernel(x, c, norm_g, ada_w, ada_b, ffn_w1, ffn_w2, a_w_in, a_b_if, a_hnorm_g, a_w_out,
           b_w_in, b_b_in, b_ln_g, b_ln_b, b_ws, b_bs, b_w_out):
    bsz, seq, d = x.shape
    assert d == D_MODEL and seq % TOKEN_TILE == 0 and norm_g.shape[0] == DEPTH
    mods = _ada(c, ada_w, ada_b).reshape(DEPTH, bsz, 6, D_MODEL)
    x2 = x.reshape(bsz * seq, D_MODEL)
    for i in range(DEPTH):
        j = i // 2
        if i % 2 == 0:
            x2 = _mlstm(x2, mods[i], norm_g[i], a_w_in[j], a_b_if[j], a_hnorm_g[j],
                        a_w_out[j], bsz, seq)
        else:
            x2 = _sgu(x2, mods[i], norm_g[i], b_w_in[j].astype(BF16), b_b_in[j], b_ln_g[j],
                      b_ln_b[j], b_ws[j], b_bs[j], b_w_out[j].astype(BF16), seq)
        x2 = _ffn(x2, mods[i], norm_g[i], ffn_w1[i].astype(BF16), ffn_w2[i].astype(BF16), seq)
    return x2.reshape(bsz, seq, D_MODEL)
```

```python
import numpy as np

import jax
import jax.numpy as jnp
from jax import lax
from jax.experimental import pallas as pl
from jax.experimental.pallas import tpu as pltpu

F32 = jnp.float32
BF16 = jnp.bfloat16

D_MODEL = 1024
DEPTH = 4
EPS = 1e-6

M_HEADS = 8
M_DV = 128
M_DK = 64
M_CHUNK = 128
GATE_CAP = 15.0
M_QK = M_HEADS * M_DK
M_QKVO = 2 * M_QK + 2 * D_MODEL

G_CHUNK = 128
G_HALF = 2048
G_GROUP = 128
G_GROUPS = 16

FFN_HIDDEN = 4096

LANES = 128
VMEM_LIMIT_BYTES = 56 * 1024 * 1024

TOKEN_TILE = 512
FFN_HIDDEN_TILE = 1024


def _dot(a, b):
    return jnp.dot(a, b, preferred_element_type=F32)


def _rms(x, g):
    return x * lax.rsqrt(jnp.mean(x * x, axis=-1, keepdims=True) + EPS) * g


def _resident(shape):
    zeros = (0,) * len(shape)
    return pl.BlockSpec(shape, lambda *_: zeros, pipeline_mode=pl.Buffered(1))


def _params(semantics):
    return pltpu.CompilerParams(dimension_semantics=semantics,
                                vmem_limit_bytes=VMEM_LIMIT_BYTES)


def _ada_kernel(c_ref, w_ref, b_ref, o_ref):
    c = c_ref[...]
    ca = (c * jax.nn.sigmoid(c)).astype(BF16)
    o_ref[...] = _dot(ca, w_ref[...].astype(BF16)) + b_ref[...]


def _ada(c, ada_w, ada_b):
    bsz = c.shape[0]
    blk = D_MODEL
    return pl.pallas_call(
        _ada_kernel,
        grid=(DEPTH, 6 * D_MODEL // blk),
        in_specs=[
            pl.BlockSpec((bsz, D_MODEL), lambda i, j: (0, 0)),
            pl.BlockSpec((None, D_MODEL, blk), lambda i, j: (i, 0, j)),
            pl.BlockSpec((None, 1, blk), lambda i, j: (i, 0, j)),
        ],
        out_specs=pl.BlockSpec((None, bsz, blk), lambda i, j: (i, 0, j)),
        out_shape=jax.ShapeDtypeStruct((DEPTH, bsz, 6 * D_MODEL), F32),
        compiler_params=_params(("arbitrary", "arbitrary")),
        name="ada_mod",
    )(c, ada_w, ada_b.reshape(DEPTH, 1, 6 * D_MODEL))


def _ffn_kernel(x_ref, mod_ref, ng_ref, w1_ref, w2_ref, o_ref):
    x = x_ref[...]
    mod = mod_ref[...]
    ng = ng_ref[...]
    h = (_rms(x, ng[2:3]) * (1.0 + mod[4:5]) + mod[3:4]).astype(BF16)
    y = jnp.zeros(x.shape, F32)
    for k0 in range(0, FFN_HIDDEN, FFN_HIDDEN_TILE):
        a = _dot(h, w1_ref[:, k0:k0 + FFN_HIDDEN_TILE])
        a = jnp.square(jnp.maximum(a, 0.0)).astype(BF16)
        y = y + _dot(a, w2_ref[k0:k0 + FFN_HIDDEN_TILE, :])
    o_ref[...] = x + mod[5:6] * _rms(y, ng[3:4])


def _ffn(x2, mod, ng, w1, w2, seq):
    tokens = x2.shape[0]
    tm = TOKEN_TILE
    per_seq = seq // tm
    return pl.pallas_call(
        _ffn_kernel,
        grid=(tokens // tm,),
        in_specs=[
            pl.BlockSpec((tm, D_MODEL), lambda i: (i, 0)),
            pl.BlockSpec((None, 6, D_MODEL), lambda i: (i // per_seq, 0, 0)),
            _resident((4, D_MODEL)),
            _resident((D_MODEL, FFN_HIDDEN)),
            _resident((FFN_HIDDEN, D_MODEL)),
        ],
        out_specs=pl.BlockSpec((tm, D_MODEL), lambda i: (i, 0)),
        out_shape=jax.ShapeDtypeStruct(x2.shape, F32),
        compiler_params=_params(("arbitrary",)),
        name="ffn_relu2",
    )(x2, mod, ng, w1, w2)


def _gelu(z):
    return 0.5 * z * (1.0 + lax.erf(z * (2.0 ** -0.5)))


def _sgu_kernel(x_ref, mod_ref, ng_ref, win_ref, bin_ref, lng_ref, lnb_ref,
                ws_ref, bsb_ref, wout_ref, o_ref, y_scr):
    x = x_ref[...]
    mod = mod_ref[...]
    ng = ng_ref[...]
    tm = x.shape[0]
    h = (_rms(x, ng[0:1]) * (1.0 + mod[1:2]) + mod[0:1]).astype(BF16)
    zu = _gelu(_dot(h, win_ref[:, :G_HALF]) + bin_ref[:, :G_HALF])
    zv = _gelu(_dot(h, win_ref[:, G_HALF:]) + bin_ref[:, G_HALF:])
    mu = jnp.mean(zv, axis=-1, keepdims=True)
    zc = zv - mu
    var = jnp.mean(zc * zc, axis=-1, keepdims=True)
    vn = (zc * lax.rsqrt(var + EPS) * lng_ref[...] + lnb_ref[...]).astype(BF16)

    row = lax.broadcasted_iota(jnp.int32, (G_CHUNK, G_CHUNK), 0)
    col = lax.broadcasted_iota(jnp.int32, (G_CHUNK, G_CHUNK), 1)
    causal = col <= row
    for g in range(G_GROUPS):
        c0 = g * G_GROUP
        wm = jnp.where(causal, ws_ref[g], 0.0).astype(BF16)
        bsg = bsb_ref[g]
        for r0 in range(0, tm, 2 * G_CHUNK):
            r1 = r0 + G_CHUNK
            r2 = r1 + G_CHUNK
            rhs = jnp.concatenate([vn[r0:r1, c0:c0 + G_GROUP],
                                   vn[r1:r2, c0:c0 + G_GROUP]], axis=1)
            sv = _dot(wm, rhs)
            y_scr[r0:r1, c0:c0 + G_GROUP] = (
                zu[r0:r1, c0:c0 + G_GROUP] * (sv[:, :G_GROUP] + bsg)).astype(BF16)
            y_scr[r1:r2, c0:c0 + G_GROUP] = (
                zu[r1:r2, c0:c0 + G_GROUP] * (sv[:, G_GROUP:] + bsg)).astype(BF16)
    y = _dot(y_scr[...], wout_ref[...])
    o_ref[...] = x + mod[2:3] * _rms(y, ng[1:2])


def _sgu(x2, mod, ng, w_in, b_in, ln_g, ln_b, ws, bs, w_out, seq):
    tokens = x2.shape[0]
    tm = TOKEN_TILE
    per_seq = seq // tm
    bsb = jnp.broadcast_to(bs[:, :, None], (G_GROUPS, G_CHUNK, LANES))
    return pl.pallas_call(
        _sgu_kernel,
        grid=(tokens // tm,),
        in_specs=[
            pl.BlockSpec((tm, D_MODEL), lambda i: (i, 0)),
            pl.BlockSpec((None, 6, D_MODEL), lambda i: (i // per_seq, 0, 0)),
            _resident((4, D_MODEL)),
            _resident((D_MODEL, 2 * G_HALF)),
            _resident((1, 2 * G_HALF)),
            _resident((1, G_HALF)),
            _resident((1, G_HALF)),
            _resident((G_GROUPS, G_CHUNK, G_CHUNK)),
            _resident((G_GROUPS, G_CHUNK, LANES)),
            _resident((G_HALF, D_MODEL)),
        ],
        out_specs=pl.BlockSpec((tm, D_MODEL), lambda i: (i, 0)),
        out_shape=jax.ShapeDtypeStruct(x2.shape, F32),
        scratch_shapes=[pltpu.VMEM((tm, G_HALF), BF16)],
        compiler_params=_params(("arbitrary",)),
        name="sgu_mixer",
    )(x2, mod, ng, w_in, b_in.reshape(1, -1), ln_g.reshape(1, -1),
      ln_b.reshape(1, -1), ws, bsb, w_out)


M_GATE_COPIES = 6
M_ARG_BLOCK = 3 * LANES
M_ARG_COLS = M_HEADS * M_ARG_BLOCK
_T_NEG_M = 0
_T_NEG_B = 3 * M_HEADS
_T_ONES = 6 * M_HEADS
_T_END = 9 * M_HEADS
_R_DATA_ROWS = 32


def _selector_const():
    r = np.zeros((LANES, M_ARG_COLS), np.float32)
    for hd in range(M_HEADS):
        c0 = hd * M_ARG_BLOCK
        for part in range(3):
            r[_T_NEG_M + part * M_HEADS + hd, c0:c0 + M_ARG_BLOCK] = 1.0
            r[_T_NEG_B + part * M_HEADS + hd, c0 + 2 * LANES:c0 + 3 * LANES] = 1.0
    return r


def _split3(v):
    hi = v.astype(BF16).astype(F32)
    r1 = v - hi
    mid = r1.astype(BF16).astype(F32)
    lo = (r1 - mid).astype(BF16).astype(F32)
    return hi, mid, lo


def _prefix_max_rows(v):
    n = v.shape[0]
    rowi = lax.broadcasted_iota(jnp.int32, v.shape, 0)
    s = 1
    while s < n:
        v = jnp.maximum(v, jnp.where(rowi >= s, pltpu.roll(v, s, axis=0), -jnp.inf))
        s *= 2
    return v


def _mlstm_kernel(x_ref, mod_ref, ng_ref, wqkvo_ref, wg_ref, bg_ref, hg_ref, sel_ref,
                  wout_ref, o_ref, p_scr, g_scr, y_scr, s_scr, msub_scr, mlane_scr,
                  rhs_scr, e_scr, a_scr, kt_scr, hh_scr):
    L = M_CHUNK

    @pl.when(pl.program_id(1) == 0)
    def _():
        s_scr[...] = jnp.zeros(s_scr.shape, F32)
        msub_scr[...] = jnp.zeros(msub_scr.shape, F32)
        mlane_scr[...] = jnp.zeros(mlane_scr.shape, F32)

    x = x_ref[...]
    mod = mod_ref[...]
    ng = ng_ref[...]
    ts = x.shape[0]
    h = (_rms(x, ng[0:1]) * (1.0 + mod[1:2]) + mod[0:1]).astype(BF16)
    p_scr[...] = _dot(h, wqkvo_ref[...])
    g_scr[...] = _dot(h, wg_ref[...]) + bg_ref[...]
    rhs_scr[...] = sel_ref[...]

    row = lax.broadcasted_iota(jnp.int32, (L, L), 0)
    col = lax.broadcasted_iota(jnp.int32, (L, L), 1)
    causal = col <= row
    tri = jnp.where(causal, 1.0, 0.0).astype(BF16)
    sub8 = lax.broadcasted_iota(jnp.int32, (M_HEADS, LANES), 0)
    ones_blk = jnp.ones((L, M_DV), BF16)
    zeros_k = jnp.zeros((M_DK, L), F32)
    zeros_s = jnp.zeros((M_DK, 2 * M_DV), F32)
    zeros_r = jnp.zeros((M_HEADS, LANES), F32)

    def chunk(c, carry):
        r0 = pl.multiple_of(c * L, L)
        rows = pl.ds(r0, L)

        ii = GATE_CAP * jnp.tanh(g_scr[rows, 0:LANES] / GATE_CAP)
        logf = jax.nn.log_sigmoid(GATE_CAP * jnp.tanh(g_scr[rows, LANES:2 * LANES] / GATE_CAP))
        f_hi, f_mid, f_lo = _split3(logf)
        bcum = (_dot(tri, f_hi.astype(BF16)) + _dot(tri, f_mid.astype(BF16))
                + _dot(tri, f_lo.astype(BF16)))
        r = ii - bcum
        cmax = _prefix_max_rows(r)
        m_lane = mlane_scr[0:1, :]
        mx = jnp.maximum(cmax, m_lane)
        b_last_l = bcum[L - 1:L, :]
        m_new_l = jnp.maximum(b_last_l + m_lane, b_last_l + cmax[L - 1:L, :])
        mlane_scr[...] = jnp.broadcast_to(m_new_l, mlane_scr.shape)

        lane = lax.broadcasted_iota(jnp.int32, (L, LANES), 1)
        nm = _split3(-mx)
        nb = _split3(-bcum)
        t = jnp.where(lane < _T_END, 1.0, 0.0)
        for part in (2, 1, 0):
            t = jnp.where(lane < _T_NEG_B + (part + 1) * M_HEADS, nb[part], t)
        for part in (2, 1, 0):
            t = jnp.where(lane < _T_NEG_M + (part + 1) * M_HEADS, nm[part], t)

        r_row = r.T[0:M_HEADS, :]
        bc_row = bcum.T[0:M_HEADS, :]
        rmax = jnp.max(r_row, axis=1, keepdims=True)
        wgt_row = jnp.exp(r_row - rmax)
        b_last = bc_row[:, L - 1:L]
        m_st = msub_scr[...]
        m_loc = b_last + rmax
        m_new = jnp.maximum(b_last + m_st, m_loc)
        s_old = jnp.exp(b_last + m_st - m_new)
        s_new = jnp.exp(m_loc - m_new)
        msub_scr[...] = m_new

        r_parts = _split3(r_row)
        m_parts = _split3(m_st)
        data = []
        for part in range(3):
            blocks = []
            for hd in range(M_HEADS):
                blocks += [jnp.where(sub8 == hd, r_parts[part], 0.0),
                           jnp.where(sub8 == hd, m_parts[part], 0.0), zeros_r]
            data.append(jnp.concatenate(blocks, axis=1))
        data.append(jnp.zeros((M_HEADS, M_ARG_COLS), F32))
        rhs_scr[_T_ONES:_T_ONES + _R_DATA_ROWS, :] = jnp.concatenate(data, axis=0).astype(BF16)
        e_scr[...] = jnp.exp(_dot(t.astype(BF16), rhs_scr[...]))

        for pair in range(M_HEADS // 2):
            pc = pair * LANES
            q2f = p_scr[rows, pc:pc + LANES]
            kT = (p_scr[rows, M_QK + pc:M_QK + pc + LANES] * (M_DK ** -0.5)).T
            kt_scr[pair] = kT
            bd = jnp.concatenate(
                [jnp.concatenate([kT[0:M_DK], zeros_k], axis=0),
                 jnp.concatenate([zeros_k, kT[M_DK:2 * M_DK]], axis=0)], axis=1)
            s_pair = _dot(q2f.astype(BF16), bd.astype(BF16))
            for half in range(2):
                hd = 2 * pair + half
                e0 = hd * M_ARG_BLOCK
                decay = jnp.where(causal, e_scr[:, e0:e0 + LANES], 0.0)
                scores = (s_pair[:, half * L:(half + 1) * L] * decay).astype(BF16)
                qi = (q2f * e_scr[:, e0 + LANES:e0 + 2 * LANES]).astype(BF16)
                a_scr[hd] = jnp.concatenate([scores, qi], axis=1)

        for hd in range(M_HEADS):
            pair, half = divmod(hd, 2)
            e0 = hd * M_ARG_BLOCK
            vc = 2 * M_QK + hd * M_DV
            vext = jnp.concatenate([p_scr[rows, vc:vc + M_DV].astype(BF16), ones_blk], axis=1)
            s_h = s_scr[pair, half * M_DK:(half + 1) * M_DK, :]
            sz = jnp.concatenate([s_h, zeros_s] if half == 0 else [zeros_s, s_h], axis=0)
            nd = _dot(a_scr[hd], jnp.concatenate([vext, sz.astype(BF16)], axis=0))
            hh_scr[:, hd * M_DV:(hd + 1) * M_DV] = nd[:, :M_DV] / jnp.maximum(
                jnp.abs(nd[:, M_DV:]), e_scr[:, e0 + 2 * LANES:e0 + 3 * LANES])

        for hd in range(M_HEADS):
            hc = hd * M_DV
            oc = 2 * M_QK + D_MODEL + hc
            hh = hh_scr[:, hc:hc + M_DV]
            hh = hh * lax.rsqrt(jnp.mean(hh * hh, axis=-1, keepdims=True) + EPS)
            hh = hh * hg_ref[:, hc:hc + M_DV]
            y_scr[rows, hc:hc + M_DV] = (jax.nn.sigmoid(p_scr[rows, oc:oc + M_DV]) * hh).astype(BF16)

        for hd in range(M_HEADS):
            pair, half = divmod(hd, 2)
            vc = 2 * M_QK + hd * M_DV
            vext = jnp.concatenate([p_scr[rows, vc:vc + M_DV].astype(BF16), ones_blk], axis=1)
            kT_h = kt_scr[pair, half * M_DK:(half + 1) * M_DK, :]
            upd = _dot((kT_h * wgt_row[hd:hd + 1, :]).astype(BF16), vext)
            s_h = s_scr[pair, half * M_DK:(half + 1) * M_DK, :]
            s_scr[pair, half * M_DK:(half + 1) * M_DK, :] = (
                jnp.concatenate([s_old[hd:hd + 1, :], s_old[hd:hd + 1, :]], axis=1) * s_h
                + jnp.concatenate([s_new[hd:hd + 1, :], s_new[hd:hd + 1, :]], axis=1) * upd)
        return carry

    lax.fori_loop(0, ts // L, chunk, 0)
    y = _dot(y_scr[...], wout_ref[...])
    o_ref[...] = x + mod[2:3] * _rms(y, ng[1:2])


def _mlstm(x2, mod, ng, w_in, b_if, hnorm_g, w_out, bsz, seq):
    ts = TOKEN_TILE
    per_seq = seq // ts
    w_qkvo = w_in[:, :M_QKVO].astype(BF16)
    pad = LANES - M_GATE_COPIES * M_HEADS

    def gate_cols(a, lo):
        g = jnp.tile(a[..., lo:lo + M_HEADS], M_GATE_COPIES)
        return jnp.pad(g, [(0, 0)] * (g.ndim - 1) + [(0, pad)])

    w_g = jnp.concatenate([gate_cols(w_in, M_QKVO), gate_cols(w_in, M_QKVO + M_HEADS)],
                          axis=1).astype(BF16)
    b_g = jnp.concatenate([gate_cols(b_if, 0), gate_cols(b_if, M_HEADS)]).reshape(1, 2 * LANES)
    sel = jnp.asarray(_selector_const(), BF16)
    return pl.pallas_call(
        _mlstm_kernel,
        grid=(bsz, per_seq),
        in_specs=[
            pl.BlockSpec((ts, D_MODEL), lambda b, t: (b * per_seq + t, 0)),
            pl.BlockSpec((None, 6, D_MODEL), lambda b, t: (b, 0, 0)),
            _resident((4, D_MODEL)),
            _resident((D_MODEL, M_QKVO)),
            _resident((D_MODEL, 2 * LANES)),
            _resident((1, 2 * LANES)),
            _resident((1, D_MODEL)),
            _resident((LANES, M_ARG_COLS)),
            _resident((D_MODEL, D_MODEL)),
        ],
        out_specs=pl.BlockSpec((ts, D_MODEL), lambda b, t: (b * per_seq + t, 0)),
        out_shape=jax.ShapeDtypeStruct(x2.shape, F32),
        scratch_shapes=[
            pltpu.VMEM((ts, M_QKVO), F32),
            pltpu.VMEM((ts, 2 * LANES), F32),
            pltpu.VMEM((ts, D_MODEL), BF16),
            pltpu.VMEM((M_HEADS // 2, 2 * M_DK, 2 * M_DV), F32),
            pltpu.VMEM((M_HEADS, LANES), F32),
            pltpu.VMEM((M_HEADS, LANES), F32),
            pltpu.VMEM((LANES, M_ARG_COLS), BF16),
            pltpu.VMEM((M_CHUNK, M_ARG_COLS), F32),
            pltpu.VMEM((M_HEADS, M_CHUNK, 2 * LANES), BF16),
            pltpu.VMEM((M_HEADS // 2, LANES, M_CHUNK), F32),
            pltpu.VMEM((M_CHUNK, D_MODEL), F32),
        ],
        compiler_params=_params(("arbitrary", "arbitrary")),
        name="mlstm_mixer",
    )(x2, mod, ng, w_qkvo, w_g, b_g, hnorm_g.reshape(1, -1), sel, w_out.astype(BF16))


def kernel(x, c, norm_g, ada_w, ada_b, ffn_w1, ffn_w2, a_w_in, a_b_if, a_hnorm_g, a_w_out,
           b_w_in, b_b_in, b_ln_g, b_ln_b, b_ws, b_bs, b_w_out):
    bsz, seq, d = x.shape
    assert d == D_MODEL and seq % TOKEN_TILE == 0 and norm_g.shape[0] == DEPTH
    mods = _ada(c, ada_w, ada_b).reshape(DEPTH, bsz, 6, D_MODEL)
    x2 = x.reshape(bsz * seq, D_MODEL)
    for i in range(DEPTH):
        j = i // 2
        if i % 2 == 0:
            x2 = _mlstm(x2, mods[i], norm_g[i], a_w_in[j], a_b_if[j], a_hnorm_g[j],
                        a_w_out[j], bsz, seq)
        else:
            x2 = _sgu(x2, mods[i], norm_g[i], b_w_in[j].astype(BF16), b_b_in[j], b_ln_g[j],
                      b_ln_b[j], b_ws[j], b_bs[j], b_w_out[j].astype(BF16), seq)
        x2 = _ffn(x2, mods[i], norm_g[i], ffn_w1[i].astype(BF16), ffn_w2[i].astype(BF16), seq)
    return x2.reshape(bsz, seq, D_MODEL)
```

```python
import numpy as np

import jax
import jax.numpy as jnp
from jax import lax
from jax.experimental import pallas as pl
from jax.experimental.pallas import tpu as pltpu

F32 = jnp.float32
BF16 = jnp.bfloat16

D_MODEL = 1024
DEPTH = 4
EPS = 1e-6

M_HEADS = 8
M_DV = 128
M_DK = 64
M_CHUNK = 128
GATE_CAP = 15.0
M_QK = M_HEADS * M_DK
M_QKVO = 2 * M_QK + 2 * D_MODEL

G_CHUNK = 128
G_HALF = 2048
G_GROUP = 128
G_GROUPS = 16

FFN_HIDDEN = 4096

LANES = 128
VMEM_LIMIT_BYTES = 56 * 1024 * 1024

TOKEN_TILE = 512
FFN_HIDDEN_TILE = 1024


def _dot(a, b):
    return jnp.dot(a, b, preferred_element_type=F32)


def _rms(x, g):
    return x * lax.rsqrt(jnp.mean(x * x, axis=-1, keepdims=True) + EPS) * g


def _resident(shape):
    zeros = (0,) * len(shape)
    return pl.BlockSpec(shape, lambda *_: zeros, pipeline_mode=pl.Buffered(1))


def _layer(shape, layer):
    index = (layer,) + (0,) * len(shape)
    return pl.BlockSpec((None,) + tuple(shape), lambda *_: index, pipeline_mode=pl.Buffered(1))


def _params(semantics, flags=None):
    return pltpu.CompilerParams(dimension_semantics=semantics,
                                vmem_limit_bytes=VMEM_LIMIT_BYTES, flags=flags)


def _ada_kernel(c_ref, w_ref, b_ref, o_ref):
    c = c_ref[...]
    ca = (c * jax.nn.sigmoid(c)).astype(BF16)
    o_ref[...] = _dot(ca, w_ref[...].astype(BF16)) + b_ref[...]


def _ada(c, ada_w, ada_b):
    bsz = c.shape[0]
    blk = D_MODEL
    return pl.pallas_call(
        _ada_kernel,
        grid=(DEPTH, 6 * D_MODEL // blk),
        in_specs=[
            pl.BlockSpec((bsz, D_MODEL), lambda i, j: (0, 0)),
            pl.BlockSpec((None, D_MODEL, blk), lambda i, j: (i, 0, j)),
            pl.BlockSpec((None, 1, blk), lambda i, j: (i, 0, j)),
        ],
        out_specs=pl.BlockSpec((None, bsz, blk), lambda i, j: (i, 0, j)),
        out_shape=jax.ShapeDtypeStruct((DEPTH, bsz, 6 * D_MODEL), F32),
        compiler_params=_params(("arbitrary", "arbitrary")),
        name="ada_mod",
    )(c, ada_w, ada_b.reshape(DEPTH, 1, 6 * D_MODEL))


def _ffn_kernel(x_ref, mod_ref, ng_ref, w1_ref, w2_ref, o_ref):
    x = x_ref[...]
    mod = mod_ref[...]
    ng = ng_ref[...]
    h = (_rms(x, ng[2:3]) * (1.0 + mod[4:5]) + mod[3:4]).astype(BF16)
    y = jnp.zeros(x.shape, F32)
    for k0 in range(0, FFN_HIDDEN, FFN_HIDDEN_TILE):
        a = _dot(h, w1_ref[:, k0:k0 + FFN_HIDDEN_TILE])
        a = jnp.square(jnp.maximum(a, 0.0)).astype(BF16)
        y = y + _dot(a, w2_ref[k0:k0 + FFN_HIDDEN_TILE, :])
    o_ref[...] = x + mod[5:6] * _rms(y, ng[3:4])


def _ffn(x2, mods, norm_g, w1, w2, layer, seq):
    tokens = x2.shape[0]
    tm = TOKEN_TILE
    per_seq = seq // tm
    return pl.pallas_call(
        _ffn_kernel,
        grid=(tokens // tm,),
        in_specs=[
            pl.BlockSpec((tm, D_MODEL), lambda i: (i, 0)),
            pl.BlockSpec((None, None, 6, D_MODEL), lambda i: (layer, i // per_seq, 0, 0)),
            _layer((4, D_MODEL), layer),
            _layer((D_MODEL, FFN_HIDDEN), layer),
            _layer((FFN_HIDDEN, D_MODEL), layer),
        ],
        out_specs=pl.BlockSpec((tm, D_MODEL), lambda i: (i, 0)),
        out_shape=jax.ShapeDtypeStruct(x2.shape, F32),
        compiler_params=_params(("arbitrary",)),
        name="ffn_relu2",
    )(x2, mods, norm_g, w1, w2)


def _gelu(z):
    return 0.5 * z * (1.0 + lax.erf(z * (2.0 ** -0.5)))


def _sgu_kernel(x_ref, mod_ref, ng_ref, win_ref, bin_ref, lng_ref, lnb_ref,
                ws_ref, bsb_ref, wout_ref, o_ref, y_scr):
    x = x_ref[...]
    mod = mod_ref[...]
    ng = ng_ref[...]
    tm = x.shape[0]
    h = (_rms(x, ng[0:1]) * (1.0 + mod[1:2]) + mod[0:1]).astype(BF16)
    zu = _gelu(_dot(h, win_ref[:, :G_HALF]) + bin_ref[:, :G_HALF])
    zv = _gelu(_dot(h, win_ref[:, G_HALF:]) + bin_ref[:, G_HALF:])
    mu = jnp.mean(zv, axis=-1, keepdims=True)
    zc = zv - mu
    var = jnp.mean(zc * zc, axis=-1, keepdims=True)
    vn = (zc * lax.rsqrt(var + EPS) * lng_ref[...] + lnb_ref[...]).astype(BF16)

    row = lax.broadcasted_iota(jnp.int32, (G_CHUNK, G_CHUNK), 0)
    col = lax.broadcasted_iota(jnp.int32, (G_CHUNK, G_CHUNK), 1)
    causal = col <= row
    for g in range(G_GROUPS):
        c0 = g * G_GROUP
        wm = jnp.where(causal, ws_ref[g], 0.0).astype(BF16)
        bsg = bsb_ref[g]
        for r0 in range(0, tm, 2 * G_CHUNK):
            r1 = r0 + G_CHUNK
            r2 = r1 + G_CHUNK
            rhs = jnp.concatenate([vn[r0:r1, c0:c0 + G_GROUP],
                                   vn[r1:r2, c0:c0 + G_GROUP]], axis=1)
            sv = _dot(wm, rhs)
            y_scr[r0:r1, c0:c0 + G_GROUP] = (
                zu[r0:r1, c0:c0 + G_GROUP] * (sv[:, :G_GROUP] + bsg)).astype(BF16)
            y_scr[r1:r2, c0:c0 + G_GROUP] = (
                zu[r1:r2, c0:c0 + G_GROUP] * (sv[:, G_GROUP:] + bsg)).astype(BF16)
    y = _dot(y_scr[...], wout_ref[...])
    o_ref[...] = x + mod[2:3] * _rms(y, ng[1:2])


def _sgu(x2, mods, norm_g, w_in, b_in, ln_g, ln_b, ws, bsb, w_out, layer, j, seq):
    tokens = x2.shape[0]
    tm = TOKEN_TILE
    per_seq = seq // tm
    return pl.pallas_call(
        _sgu_kernel,
        grid=(tokens // tm,),
        in_specs=[
            pl.BlockSpec((tm, D_MODEL), lambda i: (i, 0)),
            pl.BlockSpec((None, None, 6, D_MODEL), lambda i: (layer, i // per_seq, 0, 0)),
            _layer((4, D_MODEL), layer),
            _layer((D_MODEL, 2 * G_HALF), j),
            _layer((1, 2 * G_HALF), j),
            _layer((1, G_HALF), j),
            _layer((1, G_HALF), j),
            _layer((G_GROUPS, G_CHUNK, G_CHUNK), j),
            _layer((G_GROUPS, G_CHUNK, LANES), j),
            _layer((G_HALF, D_MODEL), j),
        ],
        out_specs=pl.BlockSpec((tm, D_MODEL), lambda i: (i, 0)),
        out_shape=jax.ShapeDtypeStruct(x2.shape, F32),
        scratch_shapes=[pltpu.VMEM((tm, G_HALF), BF16)],
        compiler_params=_params(("arbitrary",)),
        name="sgu_mixer",
    )(x2, mods, norm_g, w_in, b_in, ln_g, ln_b, ws, bsb, w_out)


M_GATE_COPIES = 6
M_ARG_BLOCK = 3 * LANES
M_ARG_COLS = M_HEADS * M_ARG_BLOCK
_T_NEG_M = 0
_T_NEG_B = 3 * M_HEADS
_T_ONES = 6 * M_HEADS
_T_END = 9 * M_HEADS
_R_DATA_ROWS = 32


def _selector_const():
    r = np.zeros((LANES, M_ARG_COLS), np.float32)
    for hd in range(M_HEADS):
        c0 = hd * M_ARG_BLOCK
        for part in range(3):
            r[_T_NEG_M + part * M_HEADS + hd, c0:c0 + M_ARG_BLOCK] = 1.0
            r[_T_NEG_B + part * M_HEADS + hd, c0 + 2 * LANES:c0 + 3 * LANES] = 1.0
    return r


def _split3(v):
    hi = v.astype(BF16).astype(F32)
    r1 = v - hi
    mid = r1.astype(BF16).astype(F32)
    lo = (r1 - mid).astype(BF16).astype(F32)
    return hi, mid, lo


def _prefix_max_rows(v):
    n = v.shape[0]
    rowi = lax.broadcasted_iota(jnp.int32, v.shape, 0)
    s = 1
    while s < n:
        v = jnp.maximum(v, jnp.where(rowi >= s, pltpu.roll(v, s, axis=0), -jnp.inf))
        s *= 2
    return v


def _mlstm_kernel(x_ref, mod_ref, ng_ref, win_ref, wg_ref, bg_ref, hg_ref, sel_ref,
                  wout_ref, o_ref, p_scr, g_scr, y_scr, s_scr, msub_scr, mlane_scr,
                  rhs_scr, e_scr, a_scr, kt_scr, hh_scr):
    L = M_CHUNK

    @pl.when(pl.program_id(1) == 0)
    def _():
        s_scr[...] = jnp.zeros(s_scr.shape, F32)
        msub_scr[...] = jnp.zeros(msub_scr.shape, F32)
        mlane_scr[...] = jnp.zeros(mlane_scr.shape, F32)

    x = x_ref[...]
    mod = mod_ref[...]
    ng = ng_ref[...]
    ts = x.shape[0]
    h = (_rms(x, ng[0:1]) * (1.0 + mod[1:2]) + mod[0:1]).astype(BF16)
    p_scr[...] = _dot(h, win_ref[:, :M_QKVO])
    g_scr[...] = _dot(h, wg_ref[...]) + bg_ref[...]
    rhs_scr[...] = sel_ref[...]

    row = lax.broadcasted_iota(jnp.int32, (L, L), 0)
    col = lax.broadcasted_iota(jnp.int32, (L, L), 1)
    causal = col <= row
    tri = jnp.where(causal, 1.0, 0.0).astype(BF16)
    sub8 = lax.broadcasted_iota(jnp.int32, (M_HEADS, LANES), 0)
    ones_blk = jnp.ones((L, M_DV), BF16)
    zeros_k = jnp.zeros((M_DK, L), F32)
    zeros_s = jnp.zeros((M_DK, 2 * M_DV), F32)
    zeros_r = jnp.zeros((M_HEADS, LANES), F32)

    def chunk(c, carry):
        r0 = pl.multiple_of(c * L, L)
        rows = pl.ds(r0, L)

        ii = GATE_CAP * jnp.tanh(g_scr[rows, 0:LANES] / GATE_CAP)
        logf = jax.nn.log_sigmoid(GATE_CAP * jnp.tanh(g_scr[rows, LANES:2 * LANES] / GATE_CAP))
        f_hi, f_mid, f_lo = _split3(logf)
        bcum = (_dot(tri, f_hi.astype(BF16)) + _dot(tri, f_mid.astype(BF16))
                + _dot(tri, f_lo.astype(BF16)))
        r = ii - bcum
        cmax = _prefix_max_rows(r)
        m_lane = mlane_scr[0:1, :]
        mx = jnp.maximum(cmax, m_lane)
        b_last_l = bcum[L - 1:L, :]
        m_new_l = jnp.maximum(b_last_l + m_lane, b_last_l + cmax[L - 1:L, :])
        mlane_scr[...] = jnp.broadcast_to(m_new_l, mlane_scr.shape)

        lane = lax.broadcasted_iota(jnp.int32, (L, LANES), 1)
        nm = _split3(-mx)
        nb = _split3(-bcum)
        t = jnp.where(lane < _T_END, 1.0, 0.0)
        for part in (2, 1, 0):
            t = jnp.where(lane < _T_NEG_B + (part + 1) * M_HEADS, nb[part], t)
        for part in (2, 1, 0):
            t = jnp.where(lane < _T_NEG_M + (part + 1) * M_HEADS, nm[part], t)

        r_row = r.T[0:M_HEADS, :]
        bc_row = bcum.T[0:M_HEADS, :]
        rmax = jnp.max(r_row, axis=1, keepdims=True)
        wgt_row = jnp.exp(r_row - rmax)
        b_last = bc_row[:, L - 1:L]
        m_st = msub_scr[...]
        m_loc = b_last + rmax
        m_new = jnp.maximum(b_last + m_st, m_loc)
        s_old = jnp.exp(b_last + m_st - m_new)
        s_new = jnp.exp(m_loc - m_new)
        msub_scr[...] = m_new

        r_parts = _split3(r_row)
        m_parts = _split3(m_st)
        data = []
        for part in range(3):
            blocks = []
            for hd in range(M_HEADS):
                blocks += [jnp.where(sub8 == hd, r_parts[part], 0.0),
                           jnp.where(sub8 == hd, m_parts[part], 0.0), zeros_r]
            data.append(jnp.concatenate(blocks, axis=1))
        data.append(jnp.zeros((M_HEADS, M_ARG_COLS), F32))
        rhs_scr[_T_ONES:_T_ONES + _R_DATA_ROWS, :] = jnp.concatenate(data, axis=0).astype(BF16)
        e_scr[...] = jnp.exp(_dot(t.astype(BF16), rhs_scr[...]))

        for pair in range(M_HEADS // 2):
            pc = pair * LANES
            q2f = p_scr[rows, pc:pc + LANES]
            kT = (p_scr[rows, M_QK + pc:M_QK + pc + LANES] * (M_DK ** -0.5)).T
            kt_scr[pair] = kT
            bd = jnp.concatenate(
                [jnp.concatenate([kT[0:M_DK], zeros_k], axis=0),
                 jnp.concatenate([zeros_k, kT[M_DK:2 * M_DK]], axis=0)], axis=1)
            s_pair = _dot(q2f.astype(BF16), bd.astype(BF16))
            for half in range(2):
                hd = 2 * pair + half
                e0 = hd * M_ARG_BLOCK
                decay = jnp.where(causal, e_scr[:, e0:e0 + LANES], 0.0)
                scores = (s_pair[:, half * L:(half + 1) * L] * decay).astype(BF16)
                qi = (q2f * e_scr[:, e0 + LANES:e0 + 2 * LANES]).astype(BF16)
                a_scr[hd] = jnp.concatenate([scores, qi], axis=1)

        for hd in range(M_HEADS):
            pair, half = divmod(hd, 2)
            e0 = hd * M_ARG_BLOCK
            vc = 2 * M_QK + hd * M_DV
            vext = jnp.concatenate([p_scr[rows, vc:vc + M_DV].astype(BF16), ones_blk], axis=1)
            s_h = s_scr[pair, half * M_DK:(half + 1) * M_DK, :]
            sz = jnp.concatenate([s_h, zeros_s] if half == 0 else [zeros_s, s_h], axis=0)
            nd = _dot(a_scr[hd], jnp.concatenate([vext, sz.astype(BF16)], axis=0))
            hh_scr[:, hd * M_DV:(hd + 1) * M_DV] = nd[:, :M_DV] / jnp.maximum(
                jnp.abs(nd[:, M_DV:]), e_scr[:, e0 + 2 * LANES:e0 + 3 * LANES])

        for hd in range(M_HEADS):
            hc = hd * M_DV
            oc = 2 * M_QK + D_MODEL + hc
            hh = hh_scr[:, hc:hc + M_DV]
            hh = hh * lax.rsqrt(jnp.mean(hh * hh, axis=-1, keepdims=True) + EPS)
            hh = hh * hg_ref[:, hc:hc + M_DV]
            y_scr[rows, hc:hc + M_DV] = (jax.nn.sigmoid(p_scr[rows, oc:oc + M_DV]) * hh).astype(BF16)

        for hd in range(M_HEADS):
            pair, half = divmod(hd, 2)
            vc = 2 * M_QK + hd * M_DV
            vext = jnp.concatenate([p_scr[rows, vc:vc + M_DV].astype(BF16), ones_blk], axis=1)
            kT_h = kt_scr[pair, half * M_DK:(half + 1) * M_DK, :]
            upd = _dot((kT_h * wgt_row[hd:hd + 1, :]).astype(BF16), vext)
            s_h = s_scr[pair, half * M_DK:(half + 1) * M_DK, :]
            s_scr[pair, half * M_DK:(half + 1) * M_DK, :] = (
                jnp.concatenate([s_old[hd:hd + 1, :], s_old[hd:hd + 1, :]], axis=1) * s_h
                + jnp.concatenate([s_new[hd:hd + 1, :], s_new[hd:hd + 1, :]], axis=1) * upd)
        return carry

    lax.fori_loop(0, ts // L, chunk, 0, unroll=True)
    y = _dot(y_scr[...], wout_ref[...])
    o_ref[...] = x + mod[2:3] * _rms(y, ng[1:2])


def _gate_copies(a, lo):
    g = jnp.tile(a[..., lo:lo + M_HEADS], M_GATE_COPIES)
    return jnp.pad(g, [(0, 0)] * (g.ndim - 1) + [(0, LANES - M_GATE_COPIES * M_HEADS)])


def _mlstm(x2, mods, norm_g, w_in, w_g, b_g, hnorm_g, w_out, layer, j, bsz, seq):
    ts = TOKEN_TILE
    per_seq = seq // ts
    sel = jnp.asarray(_selector_const(), BF16)
    return pl.pallas_call(
        _mlstm_kernel,
        grid=(bsz, per_seq),
        in_specs=[
            pl.BlockSpec((ts, D_MODEL), lambda b, t: (b * per_seq + t, 0)),
            pl.BlockSpec((None, None, 6, D_MODEL), lambda b, t: (layer, b, 0, 0)),
            _layer((4, D_MODEL), layer),
            _layer((D_MODEL, w_in.shape[-1]), j),
            _layer((D_MODEL, 2 * LANES), j),
            _layer((1, 2 * LANES), j),
            _layer((1, D_MODEL), j),
            _resident((LANES, M_ARG_COLS)),
            _layer((D_MODEL, D_MODEL), j),
        ],
        out_specs=pl.BlockSpec((ts, D_MODEL), lambda b, t: (b * per_seq + t, 0)),
        out_shape=jax.ShapeDtypeStruct(x2.shape, F32),
        scratch_shapes=[
            pltpu.VMEM((ts, M_QKVO), F32),
            pltpu.VMEM((ts, 2 * LANES), F32),
            pltpu.VMEM((ts, D_MODEL), BF16),
            pltpu.VMEM((M_HEADS // 2, 2 * M_DK, 2 * M_DV), F32),
            pltpu.VMEM((M_HEADS, LANES), F32),
            pltpu.VMEM((M_HEADS, LANES), F32),
            pltpu.VMEM((LANES, M_ARG_COLS), BF16),
            pltpu.VMEM((M_CHUNK, M_ARG_COLS), F32),
            pltpu.VMEM((M_HEADS, M_CHUNK, 2 * LANES), BF16),
            pltpu.VMEM((M_HEADS // 2, LANES, M_CHUNK), F32),
            pltpu.VMEM((M_CHUNK, D_MODEL), F32),
        ],
        compiler_params=_params(("arbitrary", "arbitrary")),
        name="mlstm_mixer",
    )(x2, mods, norm_g, w_in, w_g, b_g, hnorm_g, sel, w_out)


def kernel(x, c, norm_g, ada_w, ada_b, ffn_w1, ffn_w2, a_w_in, a_b_if, a_hnorm_g, a_w_out,
           b_w_in, b_b_in, b_ln_g, b_ln_b, b_ws, b_bs, b_w_out):
    bsz, seq, d = x.shape
    assert d == D_MODEL and seq % TOKEN_TILE == 0 and norm_g.shape[0] == DEPTH
    mods = _ada(c, ada_w, ada_b).reshape(DEPTH, bsz, 6, D_MODEL)
    ffn_w1, ffn_w2 = ffn_w1.astype(BF16), ffn_w2.astype(BF16)
    a_w_g = jnp.concatenate([_gate_copies(a_w_in, M_QKVO), _gate_copies(a_w_in, M_QKVO + M_HEADS)],
                            axis=-1).astype(BF16)
    a_b_g = jnp.concatenate([_gate_copies(a_b_if, 0), _gate_copies(a_b_if, M_HEADS)],
                            axis=-1)[:, None, :]
    a_w_in, a_w_out = a_w_in.astype(BF16), a_w_out.astype(BF16)
    a_hnorm_g = a_hnorm_g[:, None, :]
    b_w_in, b_w_out = b_w_in.astype(BF16), b_w_out.astype(BF16)
    b_b_in, b_ln_g, b_ln_b = b_b_in[:, None, :], b_ln_g[:, None, :], b_ln_b[:, None, :]
    b_bsb = jnp.broadcast_to(b_bs[..., None], b_bs.shape + (LANES,))
    x2 = x.reshape(bsz * seq, D_MODEL)
    for i in range(DEPTH):
        j = i // 2
        if i % 2 == 0:
            x2 = _mlstm(x2, mods, norm_g, a_w_in, a_w_g, a_b_g, a_hnorm_g, a_w_out,
                        i, j, bsz, seq)
        else:
            x2 = _sgu(x2, mods, norm_g, b_w_in, b_b_in, b_ln_g, b_ln_b, b_ws, b_bsb, b_w_out,
                      i, j, seq)
        x2 = _ffn(x2, mods, norm_g, ffn_w1, ffn_w2, i, seq)
    return x2.reshape(bsz, seq, D_MODEL)
```

```python
import numpy as np

import jax
import jax.numpy as jnp
from jax import lax
from jax.experimental import pallas as pl
from jax.experimental.pallas import tpu as pltpu

F32 = jnp.float32
BF16 = jnp.bfloat16

D_MODEL = 1024
DEPTH = 4
EPS = 1e-6

M_HEADS = 8
M_DV = 128
M_DK = 64
M_CHUNK = 128
GATE_CAP = 15.0
M_QK = M_HEADS * M_DK
M_QKVO = 2 * M_QK + 2 * D_MODEL

G_CHUNK = 128
G_HALF = 2048
G_GROUP = 128
G_GROUPS = 16

FFN_HIDDEN = 4096

LANES = 128
VMEM_LIMIT_BYTES = 56 * 1024 * 1024

TOKEN_TILE = 512
FFN_HIDDEN_TILE = 1024


def _dot(a, b):
    return jnp.dot(a, b, preferred_element_type=F32)


def _wdot(a, w):
    return _dot(a, w.astype(BF16))


def _rms(x, g):
    return x * lax.rsqrt(jnp.mean(x * x, axis=-1, keepdims=True) + EPS) * g


def _resident(shape):
    zeros = (0,) * len(shape)
    return pl.BlockSpec(shape, lambda *_: zeros, pipeline_mode=pl.Buffered(1))


def _layer(shape, layer):
    index = (layer,) + (0,) * len(shape)
    return pl.BlockSpec((None,) + tuple(shape), lambda *_: index, pipeline_mode=pl.Buffered(1))


def _params(semantics, flags=None):
    return pltpu.CompilerParams(dimension_semantics=semantics,
                                vmem_limit_bytes=VMEM_LIMIT_BYTES, flags=flags)


def _ada_kernel(c_ref, w_ref, b_ref, o_ref):
    c = c_ref[...]
    ca = (c * jax.nn.sigmoid(c)).astype(BF16)
    o_ref[...] = _wdot(ca, w_ref[...]) + b_ref[...]


def _ada(c, ada_w, ada_b):
    bsz = c.shape[0]
    blk = D_MODEL
    return pl.pallas_call(
        _ada_kernel,
        grid=(DEPTH, 6 * D_MODEL // blk),
        in_specs=[
            pl.BlockSpec((bsz, D_MODEL), lambda i, j: (0, 0)),
            pl.BlockSpec((None, D_MODEL, blk), lambda i, j: (i, 0, j)),
            pl.BlockSpec((None, 1, blk), lambda i, j: (i, 0, j)),
        ],
        out_specs=pl.BlockSpec((None, bsz, blk), lambda i, j: (i, 0, j)),
        out_shape=jax.ShapeDtypeStruct((DEPTH, bsz, 6 * D_MODEL), F32),
        compiler_params=_params(("arbitrary", "arbitrary")),
        name="ada_mod",
    )(c, ada_w, ada_b.reshape(DEPTH, 1, 6 * D_MODEL))


def _ffn_kernel(x_ref, mod_ref, ng_ref, w1_ref, w2_ref, o_ref):
    x = x_ref[...]
    mod = mod_ref[...]
    ng = ng_ref[...]
    h = (_rms(x, ng[2:3]) * (1.0 + mod[4:5]) + mod[3:4]).astype(BF16)
    y = jnp.zeros(x.shape, F32)
    for k0 in range(0, FFN_HIDDEN, FFN_HIDDEN_TILE):
        a = _wdot(h, w1_ref[:, k0:k0 + FFN_HIDDEN_TILE])
        a = jnp.square(jnp.maximum(a, 0.0)).astype(BF16)
        y = y + _wdot(a, w2_ref[k0:k0 + FFN_HIDDEN_TILE, :])
    o_ref[...] = x + mod[5:6] * _rms(y, ng[3:4])


def _ffn(x2, mods, norm_g, w1, w2, layer, seq):
    tokens = x2.shape[0]
    tm = TOKEN_TILE
    per_seq = seq // tm
    return pl.pallas_call(
        _ffn_kernel,
        grid=(tokens // tm,),
        in_specs=[
            pl.BlockSpec((tm, D_MODEL), lambda i: (i, 0)),
            pl.BlockSpec((None, None, 6, D_MODEL), lambda i: (layer, i // per_seq, 0, 0)),
            _layer((4, D_MODEL), layer),
            _layer((D_MODEL, FFN_HIDDEN), layer),
            _layer((FFN_HIDDEN, D_MODEL), layer),
        ],
        out_specs=pl.BlockSpec((tm, D_MODEL), lambda i: (i, 0)),
        out_shape=jax.ShapeDtypeStruct(x2.shape, F32),
        compiler_params=_params(("arbitrary",)),
        name="ffn_relu2",
    )(x2, mods, norm_g, w1, w2)


def _gelu(z):
    return 0.5 * z * (1.0 + lax.erf(z * (2.0 ** -0.5)))


def _sgu_kernel(x_ref, mod_ref, ng_ref, win_ref, bin_ref, lng_ref, lnb_ref,
                ws_ref, bsb_ref, wout_ref, o_ref, y_scr):
    x = x_ref[...]
    mod = mod_ref[...]
    ng = ng_ref[...]
    tm = x.shape[0]
    h = (_rms(x, ng[0:1]) * (1.0 + mod[1:2]) + mod[0:1]).astype(BF16)
    zv = _gelu(_wdot(h, win_ref[:, G_HALF:]) + bin_ref[:, G_HALF:])
    pu = _wdot(h, win_ref[:, :G_HALF])
    mu = jnp.mean(zv, axis=-1, keepdims=True)
    zc = zv - mu
    var = jnp.mean(zc * zc, axis=-1, keepdims=True)
    vn = (zc * lax.rsqrt(var + EPS) * lng_ref[...] + lnb_ref[...]).astype(BF16)
    zu = _gelu(pu + bin_ref[:, :G_HALF])

    row = lax.broadcasted_iota(jnp.int32, (G_CHUNK, G_CHUNK), 0)
    col = lax.broadcasted_iota(jnp.int32, (G_CHUNK, G_CHUNK), 1)
    causal = col <= row
    for g in range(G_GROUPS):
        c0 = g * G_GROUP
        wm = jnp.where(causal, ws_ref[g], 0.0).astype(BF16)
        bsg = bsb_ref[g]
        for r0 in range(0, tm, 2 * G_CHUNK):
            r1 = r0 + G_CHUNK
            r2 = r1 + G_CHUNK
            rhs = jnp.concatenate([vn[r0:r1, c0:c0 + G_GROUP],
                                   vn[r1:r2, c0:c0 + G_GROUP]], axis=1)
            sv = _dot(wm, rhs)
            y_scr[r0:r1, c0:c0 + G_GROUP] = (
                zu[r0:r1, c0:c0 + G_GROUP] * (sv[:, :G_GROUP] + bsg)).astype(BF16)
            y_scr[r1:r2, c0:c0 + G_GROUP] = (
                zu[r1:r2, c0:c0 + G_GROUP] * (sv[:, G_GROUP:] + bsg)).astype(BF16)
    y = _wdot(y_scr[...], wout_ref[...])
    o_ref[...] = x + mod[2:3] * _rms(y, ng[1:2])


def _sgu(x2, mods, norm_g, w_in, b_in, ln_g, ln_b, ws, bsb, w_out, layer, j, seq):
    tokens = x2.shape[0]
    tm = TOKEN_TILE
    per_seq = seq // tm
    return pl.pallas_call(
        _sgu_kernel,
        grid=(tokens // tm,),
        in_specs=[
            pl.BlockSpec((tm, D_MODEL), lambda i: (i, 0)),
            pl.BlockSpec((None, None, 6, D_MODEL), lambda i: (layer, i // per_seq, 0, 0)),
            _layer((4, D_MODEL), layer),
            _layer((D_MODEL, 2 * G_HALF), j),
            _layer((1, 2 * G_HALF), j),
            _layer((1, G_HALF), j),
            _layer((1, G_HALF), j),
            _layer((G_GROUPS, G_CHUNK, G_CHUNK), j),
            _layer((G_GROUPS, G_CHUNK, LANES), j),
            _layer((G_HALF, D_MODEL), j),
        ],
        out_specs=pl.BlockSpec((tm, D_MODEL), lambda i: (i, 0)),
        out_shape=jax.ShapeDtypeStruct(x2.shape, F32),
        scratch_shapes=[pltpu.VMEM((tm, G_HALF), BF16)],
        compiler_params=_params(("arbitrary",)),
        name="sgu_mixer",
    )(x2, mods, norm_g, w_in, b_in, ln_g, ln_b, ws, bsb, w_out)


M_GATE_COPIES = 6
M_ARG_BLOCK = 3 * LANES
M_ARG_COLS = M_HEADS * M_ARG_BLOCK
_T_NEG_M = 0
_T_NEG_B = 3 * M_HEADS
_T_ONES = 6 * M_HEADS
_T_END = 9 * M_HEADS
_R_DATA_ROWS = 32


def _selector_const():
    r = np.zeros((LANES, M_ARG_COLS), np.float32)
    for hd in range(M_HEADS):
        c0 = hd * M_ARG_BLOCK
        for part in range(3):
            r[_T_NEG_M + part * M_HEADS + hd, c0:c0 + M_ARG_BLOCK] = 1.0
            r[_T_NEG_B + part * M_HEADS + hd, c0 + 2 * LANES:c0 + 3 * LANES] = 1.0
    return r


def _split3(v):
    hi = v.astype(BF16).astype(F32)
    r1 = v - hi
    mid = r1.astype(BF16).astype(F32)
    lo = (r1 - mid).astype(BF16).astype(F32)
    return hi, mid, lo


def _prefix_max_rows(v):
    n = v.shape[0]
    rowi = lax.broadcasted_iota(jnp.int32, v.shape, 0)
    s = 1
    while s < n:
        v = jnp.maximum(v, jnp.where(rowi >= s, pltpu.roll(v, s, axis=0), -jnp.inf))
        s *= 2
    return v


def _mlstm_kernel(x_ref, mod_ref, ng_ref, win_ref, wg_ref, bg_ref, hg_ref, sel_ref,
                  wout_ref, o_ref, p_scr, g_scr, y_scr, s_scr, msub_scr, mlane_scr,
                  rhs_scr, e_scr, a_scr, kt_scr, hh_scr):
    L = M_CHUNK

    @pl.when(pl.program_id(1) == 0)
    def _():
        s_scr[...] = jnp.zeros(s_scr.shape, F32)
        msub_scr[...] = jnp.zeros(msub_scr.shape, F32)
        mlane_scr[...] = jnp.zeros(mlane_scr.shape, F32)

    x = x_ref[...]
    mod = mod_ref[...]
    ng = ng_ref[...]
    ts = x.shape[0]
    h = (_rms(x, ng[0:1]) * (1.0 + mod[1:2]) + mod[0:1]).astype(BF16)
    p_scr[...] = _wdot(h, win_ref[:, :M_QKVO])
    g_scr[...] = _wdot(h, wg_ref[...]) + bg_ref[...]
    rhs_scr[...] = sel_ref[...]

    row = lax.broadcasted_iota(jnp.int32, (L, L), 0)
    col = lax.broadcasted_iota(jnp.int32, (L, L), 1)
    causal = col <= row
    tri = jnp.where(causal, 1.0, 0.0).astype(BF16)
    sub8 = lax.broadcasted_iota(jnp.int32, (M_HEADS, LANES), 0)
    ones_blk = jnp.ones((L, M_DV), BF16)
    zeros_k = jnp.zeros((M_DK, L), F32)
    zeros_s = jnp.zeros((M_DK, 2 * M_DV), F32)
    zeros_r = jnp.zeros((M_HEADS, LANES), F32)

    def chunk(c, carry):
        r0 = pl.multiple_of(c * L, L)
        rows = pl.ds(r0, L)

        ii = GATE_CAP * jnp.tanh(g_scr[rows, 0:LANES] / GATE_CAP)
        logf = jax.nn.log_sigmoid(GATE_CAP * jnp.tanh(g_scr[rows, LANES:2 * LANES] / GATE_CAP))
        f_hi, f_mid, f_lo = _split3(logf)
        bcum = (_dot(tri, f_hi.astype(BF16)) + _dot(tri, f_mid.astype(BF16))
                + _dot(tri, f_lo.astype(BF16)))
        r = ii - bcum
        cmax = _prefix_max_rows(r)
        m_lane = mlane_scr[0:1, :]
        mx = jnp.maximum(cmax, m_lane)
        b_last_l = bcum[L - 1:L, :]
        m_new_l = jnp.maximum(b_last_l + m_lane, b_last_l + cmax[L - 1:L, :])
        mlane_scr[...] = jnp.broadcast_to(m_new_l, mlane_scr.shape)

        lane = lax.broadcasted_iota(jnp.int32, (L, LANES), 1)
        nm = _split3(-mx)
        nb = _split3(-bcum)
        t = jnp.where(lane < _T_END, 1.0, 0.0)
        for part in (2, 1, 0):
            t = jnp.where(lane < _T_NEG_B + (part + 1) * M_HEADS, nb[part], t)
        for part in (2, 1, 0):
            t = jnp.where(lane < _T_NEG_M + (part + 1) * M_HEADS, nm[part], t)

        r_row = r.T[0:M_HEADS, :]
        bc_row = bcum.T[0:M_HEADS, :]
        rmax = jnp.max(r_row, axis=1, keepdims=True)
        wgt_row = jnp.exp(r_row - rmax)
        b_last = bc_row[:, L - 1:L]
        m_st = msub_scr[...]
        m_loc = b_last + rmax
        m_new = jnp.maximum(b_last + m_st, m_loc)
        s_old = jnp.exp(b_last + m_st - m_new)
        s_new = jnp.exp(m_loc - m_new)
        msub_scr[...] = m_new

        r_parts = _split3(r_row)
        m_parts = _split3(m_st)
        data = []
        for part in range(3):
            blocks = []
            for hd in range(M_HEADS):
                blocks += [jnp.where(sub8 == hd, r_parts[part], 0.0),
                           jnp.where(sub8 == hd, m_parts[part], 0.0), zeros_r]
            data.append(jnp.concatenate(blocks, axis=1))
        data.append(jnp.zeros((M_HEADS, M_ARG_COLS), F32))
        rhs_scr[_T_ONES:_T_ONES + _R_DATA_ROWS, :] = jnp.concatenate(data, axis=0).astype(BF16)
        e_scr[...] = jnp.exp(_dot(t.astype(BF16), rhs_scr[...]))

        for pair in range(M_HEADS // 2):
            pc = pair * LANES
            q2f = p_scr[rows, pc:pc + LANES]
            kT = (p_scr[rows, M_QK + pc:M_QK + pc + LANES] * (M_DK ** -0.5)).T
            kt_scr[pair] = kT
            bd = jnp.concatenate(
                [jnp.concatenate([kT[0:M_DK], zeros_k], axis=0),
                 jnp.concatenate([zeros_k, kT[M_DK:2 * M_DK]], axis=0)], axis=1)
            s_pair = _dot(q2f.astype(BF16), bd.astype(BF16))
            for half in range(2):
                hd = 2 * pair + half
                e0 = hd * M_ARG_BLOCK
                decay = jnp.where(causal, e_scr[:, e0:e0 + LANES], 0.0)
                scores = (s_pair[:, half * L:(half + 1) * L] * decay).astype(BF16)
                qi = (q2f * e_scr[:, e0 + LANES:e0 + 2 * LANES]).astype(BF16)
                a_scr[hd] = jnp.concatenate([scores, qi], axis=1)

        for hd in range(M_HEADS):
            pair, half = divmod(hd, 2)
            e0 = hd * M_ARG_BLOCK
            vc = 2 * M_QK + hd * M_DV
            vext = jnp.concatenate([p_scr[rows, vc:vc + M_DV].astype(BF16), ones_blk], axis=1)
            s_h = s_scr[pair, half * M_DK:(half + 1) * M_DK, :]
            sz = jnp.concatenate([s_h, zeros_s] if half == 0 else [zeros_s, s_h], axis=0)
            nd = _dot(a_scr[hd], jnp.concatenate([vext, sz.astype(BF16)], axis=0))
            hh_scr[:, hd * M_DV:(hd + 1) * M_DV] = nd[:, :M_DV] / jnp.maximum(
                jnp.abs(nd[:, M_DV:]), e_scr[:, e0 + 2 * LANES:e0 + 3 * LANES])

        for hd in range(M_HEADS):
            hc = hd * M_DV
            oc = 2 * M_QK + D_MODEL + hc
            hh = hh_scr[:, hc:hc + M_DV]
            hh = hh * lax.rsqrt(jnp.mean(hh * hh, axis=-1, keepdims=True) + EPS)
            hh = hh * hg_ref[:, hc:hc + M_DV]
            y_scr[rows, hc:hc + M_DV] = (jax.nn.sigmoid(p_scr[rows, oc:oc + M_DV]) * hh).astype(BF16)

        for hd in range(M_HEADS):
            pair, half = divmod(hd, 2)
            vc = 2 * M_QK + hd * M_DV
            vext = jnp.concatenate([p_scr[rows, vc:vc + M_DV].astype(BF16), ones_blk], axis=1)
            kT_h = kt_scr[pair, half * M_DK:(half + 1) * M_DK, :]
            upd = _dot((kT_h * wgt_row[hd:hd + 1, :]).astype(BF16), vext)
            s_h = s_scr[pair, half * M_DK:(half + 1) * M_DK, :]
            s_scr[pair, half * M_DK:(half + 1) * M_DK, :] = (
                jnp.concatenate([s_old[hd:hd + 1, :], s_old[hd:hd + 1, :]], axis=1) * s_h
                + jnp.concatenate([s_new[hd:hd + 1, :], s_new[hd:hd + 1, :]], axis=1) * upd)
        return carry

    lax.fori_loop(0, ts // L, chunk, 0, unroll=True)
    y = _wdot(y_scr[...], wout_ref[...])
    o_ref[...] = x + mod[2:3] * _rms(y, ng[1:2])


def _gate_copies(a, lo):
    g = jnp.tile(a[..., lo:lo + M_HEADS], M_GATE_COPIES)
    return jnp.pad(g, [(0, 0)] * (g.ndim - 1) + [(0, LANES - M_GATE_COPIES * M_HEADS)])


def _mlstm(x2, mods, norm_g, w_in, w_g, b_g, hnorm_g, w_out, layer, j, bsz, seq):
    ts = TOKEN_TILE
    per_seq = seq // ts
    sel = jnp.asarray(_selector_const(), BF16)
    return pl.pallas_call(
        _mlstm_kernel,
        grid=(bsz, per_seq),
        in_specs=[
            pl.BlockSpec((ts, D_MODEL), lambda b, t: (b * per_seq + t, 0)),
            pl.BlockSpec((None, None, 6, D_MODEL), lambda b, t: (layer, b, 0, 0)),
            _layer((4, D_MODEL), layer),
            _layer((D_MODEL, w_in.shape[-1]), j),
            _layer((D_MODEL, 2 * LANES), j),
            _layer((1, 2 * LANES), j),
            _layer((1, D_MODEL), j),
            _resident((LANES, M_ARG_COLS)),
            _layer((D_MODEL, D_MODEL), j),
        ],
        out_specs=pl.BlockSpec((ts, D_MODEL), lambda b, t: (b * per_seq + t, 0)),
        out_shape=jax.ShapeDtypeStruct(x2.shape, F32),
        scratch_shapes=[
            pltpu.VMEM((ts, M_QKVO), F32),
            pltpu.VMEM((ts, 2 * LANES), F32),
            pltpu.VMEM((ts, D_MODEL), BF16),
            pltpu.VMEM((M_HEADS // 2, 2 * M_DK, 2 * M_DV), F32),
            pltpu.VMEM((M_HEADS, LANES), F32),
            pltpu.VMEM((M_HEADS, LANES), F32),
            pltpu.VMEM((LANES, M_ARG_COLS), BF16),
            pltpu.VMEM((M_CHUNK, M_ARG_COLS), F32),
            pltpu.VMEM((M_HEADS, M_CHUNK, 2 * LANES), BF16),
            pltpu.VMEM((M_HEADS // 2, LANES, M_CHUNK), F32),
            pltpu.VMEM((M_CHUNK, D_MODEL), F32),
        ],
        compiler_params=_params(("arbitrary", "arbitrary")),
        name="mlstm_mixer",
    )(x2, mods, norm_g, w_in, w_g, b_g, hnorm_g, sel, w_out)


def kernel(x, c, norm_g, ada_w, ada_b, ffn_w1, ffn_w2, a_w_in, a_b_if, a_hnorm_g, a_w_out,
           b_w_in, b_b_in, b_ln_g, b_ln_b, b_ws, b_bs, b_w_out):
    bsz, seq, d = x.shape
    assert d == D_MODEL and seq % TOKEN_TILE == 0 and norm_g.shape[0] == DEPTH
    mods = _ada(c, ada_w, ada_b).reshape(DEPTH, bsz, 6, D_MODEL)
    a_w_g = jnp.concatenate([_gate_copies(a_w_in, M_QKVO), _gate_copies(a_w_in, M_QKVO + M_HEADS)],
                            axis=-1)
    a_b_g = jnp.concatenate([_gate_copies(a_b_if, 0), _gate_copies(a_b_if, M_HEADS)],
                            axis=-1)[:, None, :]
    a_hnorm_g = a_hnorm_g[:, None, :]
    b_b_in, b_ln_g, b_ln_b = b_b_in[:, None, :], b_ln_g[:, None, :], b_ln_b[:, None, :]
    b_bsb = jnp.broadcast_to(b_bs[..., None], b_bs.shape + (LANES,))
    x2 = x.reshape(bsz * seq, D_MODEL)
    for i in range(DEPTH):
        j = i // 2
        if i % 2 == 0:
            x2 = _mlstm(x2, mods, norm_g, a_w_in, a_w_g, a_b_g, a_hnorm_g, a_w_out,
                        i, j, bsz, seq)
        else:
            x2 = _sgu(x2, mods, norm_g, b_w_in, b_b_in, b_ln_g, b_ln_b, b_ws, b_bsb, b_w_out,
                      i, j, seq)
        x2 = _ffn(x2, mods, norm_g, ffn_w1, ffn_w2, i, seq)
    return x2.reshape(bsz, seq, D_MODEL)
```

```python
import numpy as np

import jax
import jax.numpy as jnp
from jax import lax
from jax.experimental import pallas as pl
from jax.experimental.pallas import tpu as pltpu

F32 = jnp.float32
BF16 = jnp.bfloat16

D_MODEL = 1024
DEPTH = 4
EPS = 1e-6

M_HEADS = 8
M_DV = 128
M_DK = 64
M_CHUNK = 128
GATE_CAP = 15.0
M_QK = M_HEADS * M_DK
M_QKVO = 2 * M_QK + 2 * D_MODEL

G_CHUNK = 128
G_HALF = 2048
G_GROUP = 128
G_GROUPS = 16

FFN_HIDDEN = 4096

LANES = 128
VMEM_LIMIT_BYTES = 56 * 1024 * 1024

TOKEN_TILE = 512
FFN_HIDDEN_TILE = 1024


def _dot(a, b):
    return jnp.dot(a, b, preferred_element_type=F32)


def _wdot(a, w):
    return _dot(a, w.astype(BF16))


def _rms(x, g):
    return x * lax.rsqrt(jnp.mean(x * x, axis=-1, keepdims=True) + EPS) * g


def _resident(shape):
    zeros = (0,) * len(shape)
    return pl.BlockSpec(shape, lambda *_: zeros, pipeline_mode=pl.Buffered(1))


def _layer(shape, layer):
    index = (layer,) + (0,) * len(shape)
    return pl.BlockSpec((None,) + tuple(shape), lambda *_: index, pipeline_mode=pl.Buffered(1))


def _params(semantics, flags=None):
    return pltpu.CompilerParams(dimension_semantics=semantics,
                                vmem_limit_bytes=VMEM_LIMIT_BYTES, flags=flags)


def _ada_kernel(c_ref, w_ref, b_ref, o_ref):
    c = c_ref[...]
    ca = (c * jax.nn.sigmoid(c)).astype(BF16)
    o_ref[...] = _wdot(ca, w_ref[...]) + b_ref[...]


def _ada(c, ada_w, ada_b):
    bsz = c.shape[0]
    blk = 2 * D_MODEL
    return pl.pallas_call(
        _ada_kernel,
        grid=(DEPTH, 6 * D_MODEL // blk),
        in_specs=[
            pl.BlockSpec((bsz, D_MODEL), lambda i, j: (0, 0)),
            pl.BlockSpec((None, D_MODEL, blk), lambda i, j: (i, 0, j)),
            pl.BlockSpec((None, 1, blk), lambda i, j: (i, 0, j)),
        ],
        out_specs=pl.BlockSpec((None, bsz, blk), lambda i, j: (i, 0, j)),
        out_shape=jax.ShapeDtypeStruct((DEPTH, bsz, 6 * D_MODEL), F32),
        compiler_params=_params(("arbitrary", "arbitrary")),
        name="ada_mod",
    )(c, ada_w, ada_b.reshape(DEPTH, 1, 6 * D_MODEL))


def _ffn_kernel(x_ref, mod_ref, ng_ref, w1_ref, w2_ref, o_ref):
    x = x_ref[...]
    mod = mod_ref[...]
    ng = ng_ref[...]
    h = (_rms(x, ng[2:3] * (1.0 + mod[4:5])) + mod[3:4]).astype(BF16)
    y = None
    for k0 in range(0, FFN_HIDDEN, FFN_HIDDEN_TILE):
        a = _wdot(h, w1_ref[:, k0:k0 + FFN_HIDDEN_TILE])
        a = jnp.square(jnp.maximum(a, 0.0)).astype(BF16)
        yk = _wdot(a, w2_ref[k0:k0 + FFN_HIDDEN_TILE, :])
        y = yk if y is None else y + yk
    o_ref[...] = x + _rms(y, mod[5:6] * ng[3:4])


def _ffn(x2, mods, norm_g, w1, w2, layer, seq):
    tokens = x2.shape[0]
    tm = TOKEN_TILE
    per_seq = seq // tm
    return pl.pallas_call(
        _ffn_kernel,
        grid=(tokens // tm,),
        in_specs=[
            pl.BlockSpec((tm, D_MODEL), lambda i: (i, 0)),
            pl.BlockSpec((None, None, 6, D_MODEL), lambda i: (layer, i // per_seq, 0, 0)),
            _layer((4, D_MODEL), layer),
            _layer((D_MODEL, FFN_HIDDEN), layer),
            _layer((FFN_HIDDEN, D_MODEL), layer),
        ],
        out_specs=pl.BlockSpec((tm, D_MODEL), lambda i: (i, 0)),
        out_shape=jax.ShapeDtypeStruct(x2.shape, F32),
        compiler_params=_params(("arbitrary",)),
        name="ffn_relu2",
    )(x2, mods, norm_g, w1, w2)


def _gelu_x2(z):
    return z * (1.0 + lax.erf(z * (2.0 ** -0.5)))


def _sgu_kernel(x_ref, mod_ref, ng_ref, win_ref, bin_ref, lng_ref, lnb_ref,
                ws_ref, bsb_ref, wout_ref, o_ref, y_scr):
    x = x_ref[...]
    mod = mod_ref[...]
    ng = ng_ref[...]
    tm = x.shape[0]
    h = (_rms(x, ng[0:1] * (1.0 + mod[1:2])) + mod[0:1]).astype(BF16)
    zv = _gelu_x2(_wdot(h, win_ref[:, G_HALF:]) + bin_ref[:, G_HALF:])
    mu = jnp.mean(zv, axis=-1, keepdims=True)
    zc = zv - mu
    var = jnp.mean(zc * zc, axis=-1, keepdims=True)
    vn = (zc * lax.rsqrt(var + 4.0 * EPS) * lng_ref[...] + lnb_ref[...]).astype(BF16)
    zu = _gelu_x2(_wdot(h, win_ref[:, :G_HALF]) + bin_ref[:, :G_HALF])

    row = lax.broadcasted_iota(jnp.int32, (G_CHUNK, G_CHUNK), 0)
    col = lax.broadcasted_iota(jnp.int32, (G_CHUNK, G_CHUNK), 1)
    causal = col <= row
    for g in range(G_GROUPS):
        c0 = g * G_GROUP
        wm = jnp.where(causal, 0.5 * ws_ref[g], 0.0).astype(BF16)
        bsg = bsb_ref[g]
        for r0 in range(0, tm, 2 * G_CHUNK):
            r1 = r0 + G_CHUNK
            r2 = r1 + G_CHUNK
            rhs = jnp.concatenate([vn[r0:r1, c0:c0 + G_GROUP],
                                   vn[r1:r2, c0:c0 + G_GROUP]], axis=1)
            sv = _dot(wm, rhs)
            y_scr[r0:r1, c0:c0 + G_GROUP] = (
                zu[r0:r1, c0:c0 + G_GROUP] * (sv[:, :G_GROUP] + bsg)).astype(BF16)
            y_scr[r1:r2, c0:c0 + G_GROUP] = (
                zu[r1:r2, c0:c0 + G_GROUP] * (sv[:, G_GROUP:] + bsg)).astype(BF16)
    y = _wdot(y_scr[...], wout_ref[...])
    o_ref[...] = x + _rms(y, mod[2:3] * ng[1:2])


def _sgu(x2, mods, norm_g, w_in, b_in, ln_g, ln_b, ws, bsb, w_out, layer, j, seq):
    tokens = x2.shape[0]
    tm = TOKEN_TILE
    per_seq = seq // tm
    return pl.pallas_call(
        _sgu_kernel,
        grid=(tokens // tm,),
        in_specs=[
            pl.BlockSpec((tm, D_MODEL), lambda i: (i, 0)),
            pl.BlockSpec((None, None, 6, D_MODEL), lambda i: (layer, i // per_seq, 0, 0)),
            _layer((4, D_MODEL), layer),
            _layer((D_MODEL, 2 * G_HALF), j),
            _layer((1, 2 * G_HALF), j),
            _layer((1, G_HALF), j),
            _layer((1, G_HALF), j),
            _layer((G_GROUPS, G_CHUNK, G_CHUNK), j),
            _layer((G_GROUPS, G_CHUNK, LANES), j),
            _layer((G_HALF, D_MODEL), j),
        ],
        out_specs=pl.BlockSpec((tm, D_MODEL), lambda i: (i, 0)),
        out_shape=jax.ShapeDtypeStruct(x2.shape, F32),
        scratch_shapes=[pltpu.VMEM((tm, G_HALF), BF16)],
        compiler_params=_params(("arbitrary",)),
        name="sgu_mixer",
    )(x2, mods, norm_g, w_in, b_in, ln_g, ln_b, ws, bsb, w_out)


M_GATE_COPIES = 6
M_ARG_BLOCK = 3 * LANES
M_ARG_COLS = M_HEADS * M_ARG_BLOCK
_T_NEG_M = 0
_T_NEG_B = 3 * M_HEADS
_T_ONES = 6 * M_HEADS
_T_END = 9 * M_HEADS
_R_DATA_ROWS = 32


def _selector_const():
    r = np.zeros((LANES, M_ARG_COLS), np.float32)
    for hd in range(M_HEADS):
        c0 = hd * M_ARG_BLOCK
        for part in range(3):
            r[_T_NEG_M + part * M_HEADS + hd, c0:c0 + M_ARG_BLOCK] = 1.0
            r[_T_NEG_B + part * M_HEADS + hd, c0 + 2 * LANES:c0 + 3 * LANES] = 1.0
    return r


def _split3(v):
    hi = v.astype(BF16).astype(F32)
    r1 = v - hi
    mid = r1.astype(BF16).astype(F32)
    lo = (r1 - mid).astype(BF16).astype(F32)
    return hi, mid, lo


def _prefix_max_rows(v):
    n = v.shape[0]
    rowi = lax.broadcasted_iota(jnp.int32, v.shape, 0)
    s = 1
    while s < n:
        v = jnp.maximum(v, jnp.where(rowi >= s, pltpu.roll(v, s, axis=0), -jnp.inf))
        s *= 2
    return v


def _mlstm_kernel(x_ref, mod_ref, ng_ref, win_ref, wg_ref, bg_ref, hg_ref, sel_ref,
                  wout_ref, o_ref, p_scr, g_scr, y_scr, s_scr, msub_scr, mlane_scr,
                  rhs_scr, e_scr, a_scr, kt_scr, hh_scr):
    L = M_CHUNK

    @pl.when(pl.program_id(1) == 0)
    def _():
        s_scr[...] = jnp.zeros(s_scr.shape, F32)
        msub_scr[...] = jnp.zeros(msub_scr.shape, F32)
        mlane_scr[...] = jnp.zeros(mlane_scr.shape, F32)

    x = x_ref[...]
    mod = mod_ref[...]
    ng = ng_ref[...]
    ts = x.shape[0]
    h = (_rms(x, ng[0:1] * (1.0 + mod[1:2])) + mod[0:1]).astype(BF16)
    p_scr[...] = _wdot(h, win_ref[...])
    g_scr[...] = _wdot(h, wg_ref[...]) + bg_ref[...]
    rhs_scr[...] = sel_ref[...]

    row = lax.broadcasted_iota(jnp.int32, (L, L), 0)
    col = lax.broadcasted_iota(jnp.int32, (L, L), 1)
    causal = col <= row
    tri = jnp.where(causal, 1.0, 0.0).astype(BF16)
    sub8 = lax.broadcasted_iota(jnp.int32, (M_HEADS, LANES), 0)
    ones_blk = jnp.ones((L, M_DV), BF16)
    zeros_k = jnp.zeros((M_DK, L), F32)
    zeros_s = jnp.zeros((M_DK, 2 * M_DV), F32)
    zeros_r = jnp.zeros((M_HEADS, LANES), F32)

    def chunk(c, carry):
        r0 = pl.multiple_of(c * L, L)
        rows = pl.ds(r0, L)

        ii = GATE_CAP * jnp.tanh(g_scr[rows, 0:LANES] / GATE_CAP)
        logf = jax.nn.log_sigmoid(GATE_CAP * jnp.tanh(g_scr[rows, LANES:2 * LANES] / GATE_CAP))
        f_hi, f_mid, f_lo = _split3(logf)
        bcum = (_dot(tri, f_hi.astype(BF16)) + _dot(tri, f_mid.astype(BF16))
                + _dot(tri, f_lo.astype(BF16)))
        r = ii - bcum
        cmax = _prefix_max_rows(r)
        m_lane = mlane_scr[0:1, :]
        mx = jnp.maximum(cmax, m_lane)
        b_last_l = bcum[L - 1:L, :]
        m_new_l = jnp.maximum(b_last_l + m_lane, b_last_l + cmax[L - 1:L, :])
        mlane_scr[...] = jnp.broadcast_to(m_new_l, mlane_scr.shape)

        lane = lax.broadcasted_iota(jnp.int32, (L, LANES), 1)
        nm = _split3(-mx)
        nb = _split3(-bcum)
        t = jnp.where(lane < _T_END, 1.0, 0.0)
        for part in (2, 1, 0):
            t = jnp.where(lane < _T_NEG_B + (part + 1) * M_HEADS, nb[part], t)
        for part in (2, 1, 0):
            t = jnp.where(lane < _T_NEG_M + (part + 1) * M_HEADS, nm[part], t)

        r_row = r.T[0:M_HEADS, :]
        bc_row = bcum.T[0:M_HEADS, :]
        rmax = jnp.max(r_row, axis=1, keepdims=True)
        wgt_row = jnp.exp(r_row - rmax)
        b_last = bc_row[:, L - 1:L]
        m_st = msub_scr[...]
        m_loc = b_last + rmax
        m_new = jnp.maximum(b_last + m_st, m_loc)
        s_old = jnp.exp(b_last + m_st - m_new)
        s_new = jnp.exp(m_loc - m_new)
        msub_scr[...] = m_new

        r_parts = _split3(r_row)
        m_parts = _split3(m_st)
        data = []
        for part in range(3):
            blocks = []
            for hd in range(M_HEADS):
                blocks += [jnp.where(sub8 == hd, r_parts[part], 0.0),
                           jnp.where(sub8 == hd, m_parts[part], 0.0), zeros_r]
            data.append(jnp.concatenate(blocks, axis=1))
        data.append(jnp.zeros((M_HEADS, M_ARG_COLS), F32))
        rhs_scr[_T_ONES:_T_ONES + _R_DATA_ROWS, :] = jnp.concatenate(data, axis=0).astype(BF16)
        e_scr[...] = jnp.exp(_dot(t.astype(BF16), rhs_scr[...]))

        for pair in range(M_HEADS // 2):
            pc = pair * LANES
            q2f = p_scr[rows, pc:pc + LANES]
            kT = (p_scr[rows, M_QK + pc:M_QK + pc + LANES] * (M_DK ** -0.5)).T
            kt_scr[pair] = kT
            bd = jnp.concatenate(
                [jnp.concatenate([kT[0:M_DK], zeros_k], axis=0),
                 jnp.concatenate([zeros_k, kT[M_DK:2 * M_DK]], axis=0)], axis=1)
            s_pair = _dot(q2f.astype(BF16), bd.astype(BF16))
            for half in range(2):
                hd = 2 * pair + half
                e0 = hd * M_ARG_BLOCK
                decay = jnp.where(causal, e_scr[:, e0:e0 + LANES], 0.0)
                scores = (s_pair[:, half * L:(half + 1) * L] * decay).astype(BF16)
                qi = (q2f * e_scr[:, e0 + LANES:e0 + 2 * LANES]).astype(BF16)
                a_scr[hd] = jnp.concatenate([scores, qi], axis=1)

        for hd in range(M_HEADS):
            pair, half = divmod(hd, 2)
            e0 = hd * M_ARG_BLOCK
            vc = 2 * M_QK + hd * M_DV
            vext = jnp.concatenate([p_scr[rows, vc:vc + M_DV].astype(BF16), ones_blk], axis=1)
            s_h = s_scr[pair, half * M_DK:(half + 1) * M_DK, :]
            sz = jnp.concatenate([s_h, zeros_s] if half == 0 else [zeros_s, s_h], axis=0)
            nd = _dot(a_scr[hd], jnp.concatenate([vext, sz.astype(BF16)], axis=0))
            hh_scr[:, hd * M_DV:(hd + 1) * M_DV] = nd[:, :M_DV] / jnp.maximum(
                jnp.abs(nd[:, M_DV:]), e_scr[:, e0 + 2 * LANES:e0 + 3 * LANES])

        for hd in range(M_HEADS):
            hc = hd * M_DV
            oc = 2 * M_QK + D_MODEL + hc
            hh = hh_scr[:, hc:hc + M_DV]
            hh = hh * lax.rsqrt(jnp.mean(hh * hh, axis=-1, keepdims=True) + EPS)
            hh = hh * hg_ref[:, hc:hc + M_DV]
            y_scr[rows, hc:hc + M_DV] = (jax.nn.sigmoid(p_scr[rows, oc:oc + M_DV]) * hh).astype(BF16)

        for hd in range(M_HEADS):
            pair, half = divmod(hd, 2)
            vc = 2 * M_QK + hd * M_DV
            vext = jnp.concatenate([p_scr[rows, vc:vc + M_DV].astype(BF16), ones_blk], axis=1)
            kT_h = kt_scr[pair, half * M_DK:(half + 1) * M_DK, :]
            upd = _dot((kT_h * wgt_row[hd:hd + 1, :]).astype(BF16), vext)
            s_h = s_scr[pair, half * M_DK:(half + 1) * M_DK, :]
            s_scr[pair, half * M_DK:(half + 1) * M_DK, :] = (
                jnp.concatenate([s_old[hd:hd + 1, :], s_old[hd:hd + 1, :]], axis=1) * s_h
                + jnp.concatenate([s_new[hd:hd + 1, :], s_new[hd:hd + 1, :]], axis=1) * upd)
        return carry

    lax.fori_loop(0, ts // L, chunk, 0, unroll=True)
    y = _wdot(y_scr[...], wout_ref[...])
    o_ref[...] = x + _rms(y, mod[2:3] * ng[1:2])


def _gate_copies(a, lo):
    g = jnp.tile(a[..., lo:lo + M_HEADS], M_GATE_COPIES)
    return jnp.pad(g, [(0, 0)] * (g.ndim - 1) + [(0, LANES - M_GATE_COPIES * M_HEADS)])


def _mlstm(x2, mods, norm_g, w_in, w_g, b_g, hnorm_g, w_out, layer, j, bsz, seq):
    ts = TOKEN_TILE
    per_seq = seq // ts
    sel = jnp.asarray(_selector_const(), BF16)
    return pl.pallas_call(
        _mlstm_kernel,
        grid=(bsz, per_seq),
        in_specs=[
            pl.BlockSpec((ts, D_MODEL), lambda b, t: (b * per_seq + t, 0)),
            pl.BlockSpec((None, None, 6, D_MODEL), lambda b, t: (layer, b, 0, 0)),
            _layer((4, D_MODEL), layer),
            _layer((D_MODEL, M_QKVO), j),
            _layer((D_MODEL, 2 * LANES), j),
            _layer((1, 2 * LANES), j),
            _layer((1, D_MODEL), j),
            _resident((LANES, M_ARG_COLS)),
            _layer((D_MODEL, D_MODEL), j),
        ],
        out_specs=pl.BlockSpec((ts, D_MODEL), lambda b, t: (b * per_seq + t, 0)),
        out_shape=jax.ShapeDtypeStruct(x2.shape, F32),
        scratch_shapes=[
            pltpu.VMEM((ts, M_QKVO), F32),
            pltpu.VMEM((ts, 2 * LANES), F32),
            pltpu.VMEM((ts, D_MODEL), BF16),
            pltpu.VMEM((M_HEADS // 2, 2 * M_DK, 2 * M_DV), F32),
            pltpu.VMEM((M_HEADS, LANES), F32),
            pltpu.VMEM((M_HEADS, LANES), F32),
            pltpu.VMEM((LANES, M_ARG_COLS), BF16),
            pltpu.VMEM((M_CHUNK, M_ARG_COLS), F32),
            pltpu.VMEM((M_HEADS, M_CHUNK, 2 * LANES), BF16),
            pltpu.VMEM((M_HEADS // 2, LANES, M_CHUNK), F32),
            pltpu.VMEM((M_CHUNK, D_MODEL), F32),
        ],
        compiler_params=_params(("arbitrary", "arbitrary")),
        name="mlstm_mixer",
    )(x2, mods, norm_g, w_in, w_g, b_g, hnorm_g, sel, w_out)


def kernel(x, c, norm_g, ada_w, ada_b, ffn_w1, ffn_w2, a_w_in, a_b_if, a_hnorm_g, a_w_out,
           b_w_in, b_b_in, b_ln_g, b_ln_b, b_ws, b_bs, b_w_out):
    bsz, seq, d = x.shape
    assert d == D_MODEL and seq % TOKEN_TILE == 0 and norm_g.shape[0] == DEPTH
    mods = _ada(c, ada_w, ada_b).reshape(DEPTH, bsz, 6, D_MODEL)
    a_w_g = jnp.concatenate([_gate_copies(a_w_in, M_QKVO), _gate_copies(a_w_in, M_QKVO + M_HEADS)],
                            axis=-1)
    a_b_g = jnp.concatenate([_gate_copies(a_b_if, 0), _gate_copies(a_b_if, M_HEADS)],
                            axis=-1)[:, None, :]
    a_hnorm_g = a_hnorm_g[:, None, :]
    b_b_in, b_ln_g, b_ln_b = b_b_in[:, None, :], b_ln_g[:, None, :], b_ln_b[:, None, :]
    b_bsb = jnp.broadcast_to(0.5 * b_bs[..., None], b_bs.shape + (LANES,))
    x2 = x.reshape(bsz * seq, D_MODEL)
    for i in range(DEPTH):
        j = i // 2
        if i % 2 == 0:
            x2 = _mlstm(x2, mods, norm_g, a_w_in, a_w_g, a_b_g, a_hnorm_g, a_w_out,
                        i, j, bsz, seq)
        else:
            x2 = _sgu(x2, mods, norm_g, b_w_in, b_b_in, b_ln_g, b_ln_b, b_ws, b_bsb, b_w_out,
                      i, j, seq)
        x2 = _ffn(x2, mods, norm_g, ffn_w1, ffn_w2, i, seq)
    return x2.reshape(bsz, seq, D_MODEL)
```

```python
import functools

import numpy as np

import jax
import jax.numpy as jnp
from jax import lax
from jax.experimental import pallas as pl
from jax.experimental.pallas import tpu as pltpu

F32 = jnp.float32
BF16 = jnp.bfloat16

D_MODEL = 1024
DEPTH = 4
EPS = 1e-6

M_HEADS = 8
M_DV = 128
M_DK = 64
M_CHUNK = 128
GATE_CAP = 15.0
M_QK = M_HEADS * M_DK
M_QKVO = 2 * M_QK + 2 * D_MODEL

G_CHUNK = 128
G_HALF = 2048
G_GROUP = 128
G_GROUPS = 16

FFN_HIDDEN = 4096

LANES = 128
MXU_COLS = 256
VMEM_LIMIT_BYTES = 56 * 1024 * 1024

TOKEN_TILE = 512
FFN_HIDDEN_TILE = 1024


def _dot(a, b):
    return jnp.dot(a, b, preferred_element_type=F32)


def _wdot(a, w):
    return _dot(a, w.astype(BF16))


def _rms(x, g):
    return x * lax.rsqrt(jnp.mean(x * x, axis=-1, keepdims=True) + EPS) * g


def _zero_words(v):
    u = pltpu.bitcast(v, jnp.uint32)
    acc = u[0:8]
    for r in range(8, u.shape[0], 8):
        acc = acc | u[r:r + 8]
    return (acc >> 16) >> 16


def _after(w, zero_words):
    rows = 8 * 4 // w.dtype.itemsize
    top = pltpu.bitcast(pltpu.bitcast(w[0:rows], jnp.uint32) | zero_words, w.dtype)
    return jnp.concatenate([top, w[rows:]], axis=0)


def _resident(shape):
    zeros = (0,) * len(shape)
    return pl.BlockSpec(shape, lambda *_: zeros, pipeline_mode=pl.Buffered(1))


def _layer(shape, layer):
    index = (layer,) + (0,) * len(shape)
    return pl.BlockSpec((None,) + tuple(shape), lambda *_: index, pipeline_mode=pl.Buffered(1))


def _params(semantics, flags=None):
    return pltpu.CompilerParams(dimension_semantics=semantics,
                                vmem_limit_bytes=VMEM_LIMIT_BYTES, flags=flags)


def _ada_kernel(c_ref, w_ref, b_ref, o_ref):
    c = c_ref[...]
    ca = (c * jax.nn.sigmoid(c)).astype(BF16)
    o_ref[...] = _wdot(ca, w_ref[...]) + b_ref[...]


def _ada(c, ada_w, ada_b):
    bsz = c.shape[0]
    blk = 2 * D_MODEL
    return pl.pallas_call(
        _ada_kernel,
        grid=(DEPTH, 6 * D_MODEL // blk),
        in_specs=[
            pl.BlockSpec((bsz, D_MODEL), lambda i, j: (0, 0)),
            pl.BlockSpec((None, D_MODEL, blk), lambda i, j: (i, 0, j)),
            pl.BlockSpec((None, 1, blk), lambda i, j: (i, 0, j)),
        ],
        out_specs=pl.BlockSpec((None, bsz, blk), lambda i, j: (i, 0, j)),
        out_shape=jax.ShapeDtypeStruct((DEPTH, bsz, 6 * D_MODEL), F32),
        compiler_params=_params(("arbitrary", "arbitrary")),
        name="ada_mod",
    )(c, ada_w, ada_b.reshape(DEPTH, 1, 6 * D_MODEL))


FFN_PIECE = 32


def _ffn_kernel(xn_ref, xp_ref, modn_ref, modp_ref, ng_ref, w1_ref, w2_ref, o_ref,
                h_scr, y_scr, *, n):
    i = pl.program_id(0)
    tm = xn_ref.shape[0]
    ng = ng_ref[...]

    def normalise(x_ref, mod_ref, r0):
        mod = mod_ref[...]
        x = x_ref[r0:r0 + FFN_PIECE, :]
        return (_rms(x, ng[2:3] * (1.0 + mod[4:5])) + mod[3:4]).astype(BF16)

    def finish(slot, r0):
        rows = slice(r0, r0 + FFN_PIECE)
        o = xp_ref[rows, :] + _rms(y_scr[slot, rows, :], modp_ref[5:6, :] * ng[3:4])
        o_ref[rows, :] = o
        return o

    @pl.when(i == 0)
    def _():
        for r0 in range(0, tm, FFN_PIECE):
            h_scr[0, r0:r0 + FFN_PIECE, :] = normalise(xp_ref, modp_ref, r0)
        y_scr[1] = jnp.zeros(y_scr.shape[1:], F32)

    def step(slot):
        h = h_scr[slot]
        pieces = list(range(0, tm, FFN_PIECE))
        work = []
        for r0 in pieces:
            work.append(("finish", r0))
            work.append(("normalise", r0))
        tiles_per_step = 2 * (FFN_HIDDEN // FFN_HIDDEN_TILE) * (FFN_HIDDEN_TILE // MXU_COLS)
        assert len(work) == tiles_per_step

        def anchored(w):
            kind, r0 = work.pop(0)
            if kind == "finish":
                z = _zero_words(finish(1 - slot, r0))
            else:
                hn = normalise(xn_ref, modn_ref, r0)
                h_scr[1 - slot, r0:r0 + FFN_PIECE, :] = hn
                z = _zero_words(hn)
            return _after(w, z[:, :MXU_COLS])

        y = [None] * (D_MODEL // MXU_COLS)
        for k0 in range(0, FFN_HIDDEN, FFN_HIDDEN_TILE):
            a = []
            for c0 in range(k0, k0 + FFN_HIDDEN_TILE, MXU_COLS):
                aj = _wdot(h, anchored(w1_ref[:, c0:c0 + MXU_COLS]))
                a.append(jnp.square(jnp.maximum(aj, 0.0)).astype(BF16))
            a = jnp.concatenate(a, axis=1)
            for j, c0 in enumerate(range(0, D_MODEL, MXU_COLS)):
                yk = _wdot(a, anchored(w2_ref[k0:k0 + FFN_HIDDEN_TILE, c0:c0 + MXU_COLS]))
                y[j] = yk if y[j] is None else y[j] + yk
        y_scr[slot] = jnp.concatenate(y, axis=1)

    even = lax.rem(i, 2) == 0
    pl.when((i < n) & even)(lambda: step(0))
    pl.when((i < n) & jnp.logical_not(even))(lambda: step(1))

    @pl.when(i == n)
    def _():
        for r0 in range(0, tm, FFN_PIECE):
            finish((n - 1) % 2, r0)


def _ffn(x2, mods, norm_g, w1, w2, layer, seq):
    tokens = x2.shape[0]
    tm = TOKEN_TILE
    per_seq = seq // tm
    n = tokens // tm
    nxt = lambda i: jnp.minimum(i + 1, n - 1)
    prv = lambda i: jnp.maximum(i - 1, 0)
    return pl.pallas_call(
        functools.partial(_ffn_kernel, n=n),
        grid=(n + 1,),
        in_specs=[
            pl.BlockSpec((tm, D_MODEL), lambda i: (nxt(i), 0)),
            pl.BlockSpec((tm, D_MODEL), lambda i: (prv(i), 0)),
            pl.BlockSpec((None, None, 6, D_MODEL), lambda i: (layer, nxt(i) // per_seq, 0, 0)),
            pl.BlockSpec((None, None, 6, D_MODEL), lambda i: (layer, prv(i) // per_seq, 0, 0)),
            _layer((4, D_MODEL), layer),
            _layer((D_MODEL, FFN_HIDDEN), layer),
            _layer((FFN_HIDDEN, D_MODEL), layer),
        ],
        out_specs=pl.BlockSpec((tm, D_MODEL), lambda i: (prv(i), 0)),
        out_shape=jax.ShapeDtypeStruct(x2.shape, F32),
        scratch_shapes=[pltpu.VMEM((2, tm, D_MODEL), BF16),
                        pltpu.VMEM((2, tm, D_MODEL), F32)],
        compiler_params=_params(("arbitrary",)),
        name="ffn_relu2",
    )(x2, x2, mods, mods, norm_g, w1, w2)


def _gelu_x2(z):
    return z * (1.0 + lax.erf(z * (2.0 ** -0.5)))


def _sgu_kernel(x_ref, mod_ref, ng_ref, win_ref, bin_ref, lng_ref, lnb_ref,
                ws_ref, bsb_ref, wout_ref, o_ref, y_scr):
    x = x_ref[...]
    mod = mod_ref[...]
    ng = ng_ref[...]
    tm = x.shape[0]
    h = (_rms(x, ng[0:1] * (1.0 + mod[1:2])) + mod[0:1]).astype(BF16)
    zv = _gelu_x2(_wdot(h, win_ref[:, G_HALF:]) + bin_ref[:, G_HALF:])
    mu = jnp.mean(zv, axis=-1, keepdims=True)
    zc = zv - mu
    var = jnp.mean(zc * zc, axis=-1, keepdims=True)
    vn = (zc * lax.rsqrt(var + 4.0 * EPS) * lng_ref[...] + lnb_ref[...]).astype(BF16)
    zu = _gelu_x2(_wdot(h, win_ref[:, :G_HALF]) + bin_ref[:, :G_HALF])

    row = lax.broadcasted_iota(jnp.int32, (G_CHUNK, G_CHUNK), 0)
    col = lax.broadcasted_iota(jnp.int32, (G_CHUNK, G_CHUNK), 1)
    causal = col <= row
    for g in range(G_GROUPS):
        c0 = g * G_GROUP
        wm = jnp.where(causal, 0.5 * ws_ref[g], 0.0).astype(BF16)
        bsg = bsb_ref[g]
        for r0 in range(0, tm, 2 * G_CHUNK):
            r1 = r0 + G_CHUNK
            r2 = r1 + G_CHUNK
            rhs = jnp.concatenate([vn[r0:r1, c0:c0 + G_GROUP],
                                   vn[r1:r2, c0:c0 + G_GROUP]], axis=1)
            sv = _dot(wm, rhs)
            y_scr[r0:r1, c0:c0 + G_GROUP] = (
                zu[r0:r1, c0:c0 + G_GROUP] * (sv[:, :G_GROUP] + bsg)).astype(BF16)
            y_scr[r1:r2, c0:c0 + G_GROUP] = (
                zu[r1:r2, c0:c0 + G_GROUP] * (sv[:, G_GROUP:] + bsg)).astype(BF16)
    y = _wdot(y_scr[...], wout_ref[...])
    o_ref[...] = x + _rms(y, mod[2:3] * ng[1:2])


def _sgu(x2, mods, norm_g, w_in, b_in, ln_g, ln_b, ws, bsb, w_out, layer, j, seq):
    tokens = x2.shape[0]
    tm = TOKEN_TILE
    per_seq = seq // tm
    return pl.pallas_call(
        _sgu_kernel,
        grid=(tokens // tm,),
        in_specs=[
            pl.BlockSpec((tm, D_MODEL), lambda i: (i, 0)),
            pl.BlockSpec((None, None, 6, D_MODEL), lambda i: (layer, i // per_seq, 0, 0)),
            _layer((4, D_MODEL), layer),
            _layer((D_MODEL, 2 * G_HALF), j),
            _layer((1, 2 * G_HALF), j),
            _layer((1, G_HALF), j),
            _layer((1, G_HALF), j),
            _layer((G_GROUPS, G_CHUNK, G_CHUNK), j),
            _layer((G_GROUPS, G_CHUNK, LANES), j),
            _layer((G_HALF, D_MODEL), j),
        ],
        out_specs=pl.BlockSpec((tm, D_MODEL), lambda i: (i, 0)),
        out_shape=jax.ShapeDtypeStruct(x2.shape, F32),
        scratch_shapes=[pltpu.VMEM((tm, G_HALF), BF16)],
        compiler_params=_params(("arbitrary",)),
        name="sgu_mixer",
    )(x2, mods, norm_g, w_in, b_in, ln_g, ln_b, ws, bsb, w_out)


M_GATE_COPIES = 6
M_ARG_BLOCK = 3 * LANES
M_ARG_COLS = M_HEADS * M_ARG_BLOCK
_T_NEG_M = 0
_T_NEG_B = 3 * M_HEADS
_T_ONES = 6 * M_HEADS
_T_END = 9 * M_HEADS
_R_DATA_ROWS = 32


def _selector_const():
    r = np.zeros((LANES, M_ARG_COLS), np.float32)
    for hd in range(M_HEADS):
        c0 = hd * M_ARG_BLOCK
        for part in range(3):
            r[_T_NEG_M + part * M_HEADS + hd, c0:c0 + M_ARG_BLOCK] = 1.0
            r[_T_NEG_B + part * M_HEADS + hd, c0 + 2 * LANES:c0 + 3 * LANES] = 1.0
    return r


def _split3(v):
    hi = v.astype(BF16).astype(F32)
    r1 = v - hi
    mid = r1.astype(BF16).astype(F32)
    lo = (r1 - mid).astype(BF16).astype(F32)
    return hi, mid, lo


def _prefix_max_rows(v):
    n = v.shape[0]
    rowi = lax.broadcasted_iota(jnp.int32, v.shape, 0)
    s = 1
    while s < n:
        v = jnp.maximum(v, jnp.where(rowi >= s, pltpu.roll(v, s, axis=0), -jnp.inf))
        s *= 2
    return v


def _mlstm_kernel(x_ref, mod_ref, ng_ref, win_ref, wg_ref, bg_ref, hg_ref, sel_ref,
                  wout_ref, o_ref, p_scr, g_scr, y_scr, s_scr, msub_scr, mlane_scr,
                  rhs_scr, e_scr, a_scr, kt_scr, hh_scr):
    L = M_CHUNK

    @pl.when(pl.program_id(1) == 0)
    def _():
        s_scr[...] = jnp.zeros(s_scr.shape, F32)
        msub_scr[...] = jnp.zeros(msub_scr.shape, F32)
        mlane_scr[...] = jnp.zeros(mlane_scr.shape, F32)

    x = x_ref[...]
    mod = mod_ref[...]
    ng = ng_ref[...]
    ts = x.shape[0]
    h = (_rms(x, ng[0:1] * (1.0 + mod[1:2])) + mod[0:1]).astype(BF16)
    p_scr[...] = _wdot(h, win_ref[...])
    g_scr[...] = _wdot(h, wg_ref[...]) + bg_ref[...]
    rhs_scr[...] = sel_ref[...]

    row = lax.broadcasted_iota(jnp.int32, (L, L), 0)
    col = lax.broadcasted_iota(jnp.int32, (L, L), 1)
    causal = col <= row
    tri = jnp.where(causal, 1.0, 0.0).astype(BF16)
    sub8 = lax.broadcasted_iota(jnp.int32, (M_HEADS, LANES), 0)
    ones_blk = jnp.ones((L, M_DV), BF16)
    zeros_k = jnp.zeros((M_DK, L), F32)
    zeros_s = jnp.zeros((M_DK, 2 * M_DV), F32)
    zeros_r = jnp.zeros((M_HEADS, LANES), F32)

    def chunk(c, carry):
        r0 = pl.multiple_of(c * L, L)
        rows = pl.ds(r0, L)

        ii = GATE_CAP * jnp.tanh(g_scr[rows, 0:LANES] / GATE_CAP)
        logf = jax.nn.log_sigmoid(GATE_CAP * jnp.tanh(g_scr[rows, LANES:2 * LANES] / GATE_CAP))
        f_hi, f_mid, f_lo = _split3(logf)
        bcum = (_dot(tri, f_hi.astype(BF16)) + _dot(tri, f_mid.astype(BF16))
                + _dot(tri, f_lo.astype(BF16)))
        r = ii - bcum
        cmax = _prefix_max_rows(r)
        m_lane = mlane_scr[0:1, :]
        mx = jnp.maximum(cmax, m_lane)
        b_last_l = bcum[L - 1:L, :]
        m_new_l = jnp.maximum(b_last_l + m_lane, b_last_l + cmax[L - 1:L, :])
        mlane_scr[...] = jnp.broadcast_to(m_new_l, mlane_scr.shape)

        lane = lax.broadcasted_iota(jnp.int32, (L, LANES), 1)
        nm = _split3(-mx)
        nb = _split3(-bcum)
        t = jnp.where(lane < _T_END, 1.0, 0.0)
        for part in (2, 1, 0):
            t = jnp.where(lane < _T_NEG_B + (part + 1) * M_HEADS, nb[part], t)
        for part in (2, 1, 0):
            t = jnp.where(lane < _T_NEG_M + (part + 1) * M_HEADS, nm[part], t)

        r_row = r.T[0:M_HEADS, :]
        bc_row = bcum.T[0:M_HEADS, :]
        rmax = jnp.max(r_row, axis=1, keepdims=True)
        wgt_row = jnp.exp(r_row - rmax)
        b_last = bc_row[:, L - 1:L]
        m_st = msub_scr[...]
        m_loc = b_last + rmax
        m_new = jnp.maximum(b_last + m_st, m_loc)
        s_old = jnp.exp(b_last + m_st - m_new)
        s_new = jnp.exp(m_loc - m_new)
        msub_scr[...] = m_new

        r_parts = _split3(r_row)
        m_parts = _split3(m_st)
        data = []
        for part in range(3):
            blocks = []
            for hd in range(M_HEADS):
                blocks += [jnp.where(sub8 == hd, r_parts[part], 0.0),
                           jnp.where(sub8 == hd, m_parts[part], 0.0), zeros_r]
            data.append(jnp.concatenate(blocks, axis=1))
        data.append(jnp.zeros((M_HEADS, M_ARG_COLS), F32))
        rhs_scr[_T_ONES:_T_ONES + _R_DATA_ROWS, :] = jnp.concatenate(data, axis=0).astype(BF16)
        e_scr[...] = jnp.exp(_dot(t.astype(BF16), rhs_scr[...]))

        for pair in range(M_HEADS // 2):
            pc = pair * LANES
            q2f = p_scr[rows, pc:pc + LANES]
            kT = (p_scr[rows, M_QK + pc:M_QK + pc + LANES] * (M_DK ** -0.5)).T
            kt_scr[pair] = kT
            bd = jnp.concatenate(
                [jnp.concatenate([kT[0:M_DK], zeros_k], axis=0),
                 jnp.concatenate([zeros_k, kT[M_DK:2 * M_DK]], axis=0)], axis=1)
            s_pair = _dot(q2f.astype(BF16), bd.astype(BF16))
            for half in range(2):
                hd = 2 * pair + half
                e0 = hd * M_ARG_BLOCK
                decay = jnp.where(causal, e_scr[:, e0:e0 + LANES], 0.0)
                scores = (s_pair[:, half * L:(half + 1) * L] * decay).astype(BF16)
                qi = (q2f * e_scr[:, e0 + LANES:e0 + 2 * LANES]).astype(BF16)
                a_scr[hd] = jnp.concatenate([scores, qi], axis=1)

        for hd in range(M_HEADS):
            pair, half = divmod(hd, 2)
            e0 = hd * M_ARG_BLOCK
            vc = 2 * M_QK + hd * M_DV
            vext = jnp.concatenate([p_scr[rows, vc:vc + M_DV].astype(BF16), ones_blk], axis=1)
            s_h = s_scr[pair, half * M_DK:(half + 1) * M_DK, :]
            sz = jnp.concatenate([s_h, zeros_s] if half == 0 else [zeros_s, s_h], axis=0)
            nd = _dot(a_scr[hd], jnp.concatenate([vext, sz.astype(BF16)], axis=0))
            hh_scr[:, hd * M_DV:(hd + 1) * M_DV] = nd[:, :M_DV] / jnp.maximum(
                jnp.abs(nd[:, M_DV:]), e_scr[:, e0 + 2 * LANES:e0 + 3 * LANES])

        for hd in range(M_HEADS):
            hc = hd * M_DV
            oc = 2 * M_QK + D_MODEL + hc
            hh = hh_scr[:, hc:hc + M_DV]
            hh = hh * lax.rsqrt(jnp.mean(hh * hh, axis=-1, keepdims=True) + EPS)
            hh = hh * hg_ref[:, hc:hc + M_DV]
            y_scr[rows, hc:hc + M_DV] = (jax.nn.sigmoid(p_scr[rows, oc:oc + M_DV]) * hh).astype(BF16)

        for hd in range(M_HEADS):
            pair, half = divmod(hd, 2)
            vc = 2 * M_QK + hd * M_DV
            vext = jnp.concatenate([p_scr[rows, vc:vc + M_DV].astype(BF16), ones_blk], axis=1)
            kT_h = kt_scr[pair, half * M_DK:(half + 1) * M_DK, :]
            upd = _dot((kT_h * wgt_row[hd:hd + 1, :]).astype(BF16), vext)
            s_h = s_scr[pair, half * M_DK:(half + 1) * M_DK, :]
            s_scr[pair, half * M_DK:(half + 1) * M_DK, :] = (
                jnp.concatenate([s_old[hd:hd + 1, :], s_old[hd:hd + 1, :]], axis=1) * s_h
                + jnp.concatenate([s_new[hd:hd + 1, :], s_new[hd:hd + 1, :]], axis=1) * upd)
        return carry

    lax.fori_loop(0, ts // L, chunk, 0, unroll=True)
    y = _wdot(y_scr[...], wout_ref[...])
    o_ref[...] = x + _rms(y, mod[2:3] * ng[1:2])


def _gate_copies(a, lo):
    g = jnp.tile(a[..., lo:lo + M_HEADS], M_GATE_COPIES)
    return jnp.pad(g, [(0, 0)] * (g.ndim - 1) + [(0, LANES - M_GATE_COPIES * M_HEADS)])


def _mlstm(x2, mods, norm_g, w_in, w_g, b_g, hnorm_g, w_out, layer, j, bsz, seq):
    ts = TOKEN_TILE
    per_seq = seq // ts
    sel = jnp.asarray(_selector_const(), BF16)
    return pl.pallas_call(
        _mlstm_kernel,
        grid=(bsz, per_seq),
        in_specs=[
            pl.BlockSpec((ts, D_MODEL), lambda b, t: (b * per_seq + t, 0)),
            pl.BlockSpec((None, None, 6, D_MODEL), lambda b, t: (layer, b, 0, 0)),
            _layer((4, D_MODEL), layer),
            _layer((D_MODEL, M_QKVO), j),
            _layer((D_MODEL, 2 * LANES), j),
            _layer((1, 2 * LANES), j),
            _layer((1, D_MODEL), j),
            _resident((LANES, M_ARG_COLS)),
            _layer((D_MODEL, D_MODEL), j),
        ],
        out_specs=pl.BlockSpec((ts, D_MODEL), lambda b, t: (b * per_seq + t, 0)),
        out_shape=jax.ShapeDtypeStruct(x2.shape, F32),
        scratch_shapes=[
            pltpu.VMEM((ts, M_QKVO), F32),
            pltpu.VMEM((ts, 2 * LANES), F32),
            pltpu.VMEM((ts, D_MODEL), BF16),
            pltpu.VMEM((M_HEADS // 2, 2 * M_DK, 2 * M_DV), F32),
            pltpu.VMEM((M_HEADS, LANES), F32),
            pltpu.VMEM((M_HEADS, LANES), F32),
            pltpu.VMEM((LANES, M_ARG_COLS), BF16),
            pltpu.VMEM((M_CHUNK, M_ARG_COLS), F32),
            pltpu.VMEM((M_HEADS, M_CHUNK, 2 * LANES), BF16),
            pltpu.VMEM((M_HEADS // 2, LANES, M_CHUNK), F32),
            pltpu.VMEM((M_CHUNK, D_MODEL), F32),
        ],
        compiler_params=_params(("arbitrary", "arbitrary")),
        name="mlstm_mixer",
    )(x2, mods, norm_g, w_in, w_g, b_g, hnorm_g, sel, w_out)


def kernel(x, c, norm_g, ada_w, ada_b, ffn_w1, ffn_w2, a_w_in, a_b_if, a_hnorm_g, a_w_out,
           b_w_in, b_b_in, b_ln_g, b_ln_b, b_ws, b_bs, b_w_out):
    bsz, seq, d = x.shape
    assert d == D_MODEL and seq % TOKEN_TILE == 0 and norm_g.shape[0] == DEPTH
    mods = _ada(c, ada_w, ada_b).reshape(DEPTH, bsz, 6, D_MODEL)
    a_w_g = jnp.concatenate([_gate_copies(a_w_in, M_QKVO), _gate_copies(a_w_in, M_QKVO + M_HEADS)],
                            axis=-1)
    a_b_g = jnp.concatenate([_gate_copies(a_b_if, 0), _gate_copies(a_b_if, M_HEADS)],
                            axis=-1)[:, None, :]
    a_hnorm_g = a_hnorm_g[:, None, :]
    b_b_in, b_ln_g, b_ln_b = b_b_in[:, None, :], b_ln_g[:, None, :], b_ln_b[:, None, :]
    b_bsb = jnp.broadcast_to(0.5 * b_bs[..., None], b_bs.shape + (LANES,))
    x2 = x.reshape(bsz * seq, D_MODEL)
    for i in range(DEPTH):
        j = i // 2
        if i % 2 == 0:
            x2 = _mlstm(x2, mods, norm_g, a_w_in, a_w_g, a_b_g, a_hnorm_g, a_w_out,
                        i, j, bsz, seq)
        else:
            x2 = _sgu(x2, mods, norm_g, b_w_in, b_b_in, b_ln_g, b_ln_b, b_ws, b_bsb, b_w_out,
                      i, j, seq)
        x2 = _ffn(x2, mods, norm_g, ffn_w1, ffn_w2, i, seq)
    return x2.reshape(bsz, seq, D_MODEL)
```

```python
import functools

import numpy as np

import jax
import jax.numpy as jnp
from jax import lax
from jax.experimental import pallas as pl
from jax.experimental.pallas import tpu as pltpu

F32 = jnp.float32
BF16 = jnp.bfloat16

D_MODEL = 1024
DEPTH = 4
EPS = 1e-6

M_HEADS = 8
M_DV = 128
M_DK = 64
M_CHUNK = 128
GATE_CAP = 15.0
LOG2_E = 1.4426950408889634
M_QK = M_HEADS * M_DK
M_QKVO = 2 * M_QK + 2 * D_MODEL

G_CHUNK = 128
G_HALF = 2048
G_GROUP = 128
G_GROUPS = 16

FFN_HIDDEN = 4096

LANES = 128
MXU_COLS = 256
VMEM_LIMIT_BYTES = 60 * 1024 * 1024

TOKEN_TILE = 512
FFN_HIDDEN_TILE = 1024


def _dot(a, b):
    return jnp.dot(a, b, preferred_element_type=F32)


def _wdot(a, w):
    return _dot(a, w.astype(BF16))


def _rms(x, g):
    return x * lax.rsqrt(jnp.mean(x * x, axis=-1, keepdims=True) + EPS) * g


def _zero_words(v):
    u = pltpu.bitcast(v, jnp.uint32)
    acc = u[0:8]
    for r in range(8, u.shape[0], 8):
        acc = acc | u[r:r + 8]
    return (acc >> 16) >> 16


def _after(w, zero_words):
    rows = 8 * 4 // w.dtype.itemsize
    k = zero_words.shape[1]
    top = pltpu.bitcast(pltpu.bitcast(w[0:rows, 0:k], jnp.uint32) | zero_words, w.dtype)
    if k < w.shape[1]:
        top = jnp.concatenate([top, w[0:rows, k:]], axis=1)
    return jnp.concatenate([top, w[rows:]], axis=0)


def _resident(shape):
    zeros = (0,) * len(shape)
    return pl.BlockSpec(shape, lambda *_: zeros, pipeline_mode=pl.Buffered(1))


def _layer(shape, layer):
    index = (layer,) + (0,) * len(shape)
    return pl.BlockSpec((None,) + tuple(shape), lambda *_: index, pipeline_mode=pl.Buffered(1))


def _params(semantics, flags=None):
    return pltpu.CompilerParams(dimension_semantics=semantics,
                                vmem_limit_bytes=VMEM_LIMIT_BYTES, flags=flags)


def _ada_kernel(c_ref, w_ref, b_ref, o_ref):
    c = c_ref[...]
    ca = (c * jax.nn.sigmoid(c)).astype(BF16)
    o_ref[...] = _wdot(ca, w_ref[...]) + b_ref[...]


def _ada(c, ada_w, ada_b):
    bsz = c.shape[0]
    blk = 2 * D_MODEL
    return pl.pallas_call(
        _ada_kernel,
        grid=(DEPTH, 6 * D_MODEL // blk),
        in_specs=[
            pl.BlockSpec((bsz, D_MODEL), lambda i, j: (0, 0)),
            pl.BlockSpec((None, D_MODEL, blk), lambda i, j: (i, 0, j)),
            pl.BlockSpec((None, 1, blk), lambda i, j: (i, 0, j)),
        ],
        out_specs=pl.BlockSpec((None, bsz, blk), lambda i, j: (i, 0, j)),
        out_shape=jax.ShapeDtypeStruct((DEPTH, bsz, 6 * D_MODEL), F32),
        compiler_params=_params(("arbitrary", "arbitrary")),
        name="ada_mod",
    )(c, ada_w, ada_b.reshape(DEPTH, 1, 6 * D_MODEL))


_MIXER_ROWS = (0, 0, 1, 1, 2)
_FFN_ROWS = (2, 3, 4, 3, 5)


def _pipelined(n, piece, tiles, refs, rows, core, start_fetch):
    xn_ref, xp_ref, modn_ref, modp_ref, ng_ref, o_ref, h_scr, y_scr = refs
    g_in, shift, scale, g_out, gate = rows
    i = pl.program_id(0)
    tm = xn_ref.shape[0]
    ng = ng_ref[...]

    def normalise(x_ref, mod_ref, slot, r0):
        mod = mod_ref[...]
        gain = ng[g_in:g_in + 1] * (1.0 + mod[scale:scale + 1])
        hn = (_rms(x_ref[r0:r0 + piece, :], gain) + mod[shift:shift + 1]).astype(BF16)
        h_scr[slot, r0:r0 + piece, :] = hn
        return hn

    def finish(slot, r0):
        gain = modp_ref[gate:gate + 1, :] * ng[g_out:g_out + 1]
        o = xp_ref[r0:r0 + piece, :] + _rms(y_scr[slot, r0:r0 + piece, :], gain)
        o_ref[r0:r0 + piece, :] = o
        return o

    @pl.when(i == 0)
    def _():
        start_fetch()
        for r0 in range(0, tm, piece):
            normalise(xp_ref, modp_ref, 0, r0)
        y_scr[1] = jnp.zeros(y_scr.shape[1:], F32)

    def step(slot, first):
        work = []
        for r0 in range(0, tm, piece):
            work.append(functools.partial(finish, 1 - slot, r0))
            work.append(functools.partial(normalise, xn_ref, modn_ref, 1 - slot, r0))

        left = [tiles]

        def anchored(w):
            take = -(-len(work) // left[0])
            left[0] -= 1
            for _ in range(take):
                w = _after(w, _zero_words(work.pop(0)())[:, :w.shape[1]])
            return w

        core(slot, anchored, first)
        assert not work and left[0] == 0, "core must pass exactly `tiles` weight tiles"

    even = lax.rem(i, 2) == 0
    pl.when(i == 0)(lambda: step(0, True))
    pl.when((i > 0) & (i < n) & even)(lambda: step(0, False))
    pl.when((i < n) & jnp.logical_not(even))(lambda: step(1, False))

    @pl.when(i == n)
    def _():
        for r0 in range(0, tm, piece):
            finish((n - 1) % 2, r0)


def _pipelined_specs(n, tm, per_seq, layer):
    nxt = lambda i: jnp.minimum(i + 1, n - 1)
    prv = lambda i: jnp.maximum(i - 1, 0)
    in_specs = [
        pl.BlockSpec((tm, D_MODEL), lambda i: (nxt(i), 0)),
        pl.BlockSpec((tm, D_MODEL), lambda i: (prv(i), 0)),
        pl.BlockSpec((None, None, 6, D_MODEL), lambda i: (layer, nxt(i) // per_seq, 0, 0)),
        pl.BlockSpec((None, None, 6, D_MODEL), lambda i: (layer, prv(i) // per_seq, 0, 0)),
        _layer((4, D_MODEL), layer),
    ]
    return in_specs, pl.BlockSpec((tm, D_MODEL), lambda i: (prv(i), 0))


def _pipelined_scratch(tm):
    return [pltpu.VMEM((2, tm, D_MODEL), BF16),
            pltpu.VMEM((2, tm, D_MODEL), F32)]


def _ffn_kernel(xn_ref, xp_ref, modn_ref, modp_ref, ng_ref, w1_hbm, w2_hbm, o_ref,
                h_scr, y_scr, w1_ref, w2_ref, sem, *, n, layer):
    chunks = list(range(0, FFN_HIDDEN, FFN_HIDDEN_TILE))

    def fetch(which, k):
        k0 = chunks[k]
        if which == 0:
            src = w1_hbm.at[layer, :, k0:k0 + FFN_HIDDEN_TILE]
            dst = w1_ref.at[:, k0:k0 + FFN_HIDDEN_TILE]
        else:
            src = w2_hbm.at[layer, k0:k0 + FFN_HIDDEN_TILE, :]
            dst = w2_ref.at[k0:k0 + FFN_HIDDEN_TILE, :]
        return pltpu.make_async_copy(src, dst, sem.at[which, k])

    def start_fetch():
        for k in range(len(chunks)):
            fetch(0, k).start()
            fetch(1, k).start()

    def core(slot, anchored, first):
        h = h_scr[slot]
        y = [None] * (D_MODEL // MXU_COLS)
        for k, k0 in enumerate(chunks):
            if first:
                fetch(0, k).wait()
            a = []
            for c0 in range(k0, k0 + FFN_HIDDEN_TILE, MXU_COLS):
                aj = _wdot(h, anchored(w1_ref[:, c0:c0 + MXU_COLS]))
                a.append(jnp.square(jnp.maximum(aj, 0.0)).astype(BF16))
            a = jnp.concatenate(a, axis=1)
            if first:
                fetch(1, k).wait()
            for j, c0 in enumerate(range(0, D_MODEL, MXU_COLS)):
                yk = _wdot(a, anchored(w2_ref[k0:k0 + FFN_HIDDEN_TILE, c0:c0 + MXU_COLS]))
                y[j] = yk if y[j] is None else y[j] + yk
        y_scr[slot] = jnp.concatenate(y, axis=1)

    tiles = (FFN_HIDDEN // MXU_COLS) + (FFN_HIDDEN // FFN_HIDDEN_TILE) * (D_MODEL // MXU_COLS)
    _pipelined(n, 32, tiles, (xn_ref, xp_ref, modn_ref, modp_ref, ng_ref, o_ref, h_scr, y_scr),
               _FFN_ROWS, core, start_fetch)


def _ffn(x2, mods, norm_g, w1, w2, layer, seq):
    tokens = x2.shape[0]
    tm = TOKEN_TILE
    per_seq = seq // tm
    n = tokens // tm
    in_specs, out_spec = _pipelined_specs(n, tm, per_seq, layer)
    return pl.pallas_call(
        functools.partial(_ffn_kernel, n=n, layer=layer),
        grid=(n + 1,),
        in_specs=in_specs + [
            pl.BlockSpec(memory_space=pl.ANY),
            pl.BlockSpec(memory_space=pl.ANY),
        ],
        out_specs=out_spec,
        out_shape=jax.ShapeDtypeStruct(x2.shape, F32),
        scratch_shapes=_pipelined_scratch(tm) + [
            pltpu.VMEM((D_MODEL, FFN_HIDDEN), F32),
            pltpu.VMEM((FFN_HIDDEN, D_MODEL), F32),
            pltpu.SemaphoreType.DMA((2, FFN_HIDDEN // FFN_HIDDEN_TILE)),
        ],
        compiler_params=_params(("arbitrary",)),
        name="ffn_relu2",
    )(x2, x2, mods, mods, norm_g, w1, w2)


def _gelu_x2(z):
    return z * (1.0 + lax.erf(z * (2.0 ** -0.5)))


def _sgu_kernel(x_ref, mod_ref, ng_ref, win_ref, bin_ref, lng_ref, lnb_ref,
                ws_ref, bsb_ref, wout_ref, o_ref, y_scr):
    x = x_ref[...]
    mod = mod_ref[...]
    ng = ng_ref[...]
    tm = x.shape[0]
    h = (_rms(x, ng[0:1] * (1.0 + mod[1:2])) + mod[0:1]).astype(BF16)
    zv = _gelu_x2(_wdot(h, win_ref[:, G_HALF:]) + bin_ref[:, G_HALF:])
    mu = jnp.mean(zv, axis=-1, keepdims=True)
    zc = zv - mu
    var = jnp.mean(zc * zc, axis=-1, keepdims=True)
    vn = (zc * lax.rsqrt(var + 4.0 * EPS) * lng_ref[...] + lnb_ref[...]).astype(BF16)
    zu = _gelu_x2(_wdot(h, win_ref[:, :G_HALF]) + bin_ref[:, :G_HALF])

    row = lax.broadcasted_iota(jnp.int32, (G_CHUNK, G_CHUNK), 0)
    col = lax.broadcasted_iota(jnp.int32, (G_CHUNK, G_CHUNK), 1)
    causal = col <= row
    for g in range(G_GROUPS):
        c0 = g * G_GROUP
        wm = jnp.where(causal, 0.5 * ws_ref[g], 0.0).astype(BF16)
        bsg = bsb_ref[g]
        for r0 in range(0, tm, 2 * G_CHUNK):
            r1 = r0 + G_CHUNK
            r2 = r1 + G_CHUNK
            rhs = jnp.concatenate([vn[r0:r1, c0:c0 + G_GROUP],
                                   vn[r1:r2, c0:c0 + G_GROUP]], axis=1)
            sv = _dot(wm, rhs)
            y_scr[r0:r1, c0:c0 + G_GROUP] = (
                zu[r0:r1, c0:c0 + G_GROUP] * (sv[:, :G_GROUP] + bsg)).astype(BF16)
            y_scr[r1:r2, c0:c0 + G_GROUP] = (
                zu[r1:r2, c0:c0 + G_GROUP] * (sv[:, G_GROUP:] + bsg)).astype(BF16)
    y = _wdot(y_scr[...], wout_ref[...])
    o_ref[...] = x + _rms(y, mod[2:3] * ng[1:2])


def _sgu(x2, mods, norm_g, w_in, b_in, ln_g, ln_b, ws, bsb, w_out, layer, j, seq):
    tokens = x2.shape[0]
    tm = TOKEN_TILE
    per_seq = seq // tm
    return pl.pallas_call(
        _sgu_kernel,
        grid=(tokens // tm,),
        in_specs=[
            pl.BlockSpec((tm, D_MODEL), lambda i: (i, 0)),
            pl.BlockSpec((None, None, 6, D_MODEL), lambda i: (layer, i // per_seq, 0, 0)),
            _layer((4, D_MODEL), layer),
            _layer((D_MODEL, 2 * G_HALF), j),
            _layer((1, 2 * G_HALF), j),
            _layer((1, G_HALF), j),
            _layer((1, G_HALF), j),
            _layer((G_GROUPS, G_CHUNK, G_CHUNK), j),
            _layer((G_GROUPS, G_CHUNK, LANES), j),
            _layer((G_HALF, D_MODEL), j),
        ],
        out_specs=pl.BlockSpec((tm, D_MODEL), lambda i: (i, 0)),
        out_shape=jax.ShapeDtypeStruct(x2.shape, F32),
        scratch_shapes=[pltpu.VMEM((tm, G_HALF), BF16)],
        compiler_params=_params(("arbitrary",)),
        name="sgu_mixer",
    )(x2, mods, norm_g, w_in, b_in, ln_g, ln_b, ws, bsb, w_out)


M_GATE_COPIES = 6
M_ARG_BLOCK = 3 * LANES
M_ARG_COLS = M_HEADS * M_ARG_BLOCK
_T_NEG_M = 0
_T_NEG_B = 3 * M_HEADS
_T_ONES = 6 * M_HEADS
_T_END = 9 * M_HEADS
_R_DATA_ROWS = 32


def _selector_const():
    r = np.zeros((LANES, M_ARG_COLS), np.float32)
    for hd in range(M_HEADS):
        c0 = hd * M_ARG_BLOCK
        for part in range(3):
            r[_T_NEG_M + part * M_HEADS + hd, c0:c0 + M_ARG_BLOCK] = 1.0
            r[_T_NEG_B + part * M_HEADS + hd, c0 + 2 * LANES:c0 + 3 * LANES] = 1.0
    return r


def _split3(v):
    hi = v.astype(BF16).astype(F32)
    r1 = v - hi
    mid = r1.astype(BF16).astype(F32)
    lo = (r1 - mid).astype(BF16).astype(F32)
    return hi, mid, lo


def _prefix_max_rows(v):
    n = v.shape[0]
    rowi = lax.broadcasted_iota(jnp.int32, v.shape, 0)
    s = 1
    while s < n:
        v = jnp.maximum(v, jnp.where(rowi >= s, pltpu.roll(v, s, axis=0), -jnp.inf))
        s *= 2
    return v


def _mlstm_kernel(x_ref, mod_ref, ng_ref, win_ref, wg_ref, bg_ref, hg_ref, sel_ref,
                  wout_ref, o_ref, p_scr, g_scr, y_scr, s_scr, msub_scr, mlane_scr,
                  rhs_scr, e_scr, a_scr, kt_scr, hh_scr):
    L = M_CHUNK
    ts = x_ref.shape[0]

    @pl.when(pl.program_id(1) == 0)
    def _():
        s_scr[...] = jnp.zeros(s_scr.shape, F32)
        msub_scr[...] = jnp.zeros(msub_scr.shape, F32)
        mlane_scr[...] = jnp.zeros(mlane_scr.shape, F32)

    row = lax.broadcasted_iota(jnp.int32, (L, L), 0)
    col = lax.broadcasted_iota(jnp.int32, (L, L), 1)
    causal = col <= row
    tri = jnp.where(causal, 1.0, 0.0).astype(BF16)
    sub8 = lax.broadcasted_iota(jnp.int32, (M_HEADS, LANES), 0)
    ones_blk = jnp.ones((L, M_DV), BF16)
    zeros_k = jnp.zeros((M_DK, L), F32)
    zeros_s = jnp.zeros((M_DK, 2 * M_DV), F32)
    zeros_r = jnp.zeros((M_HEADS, LANES), F32)

    def chunk(c):
        rows = slice(c * L, (c + 1) * L)

        ii = GATE_CAP * jnp.tanh(g_scr[rows, 0:LANES] / GATE_CAP)
        logf = jax.nn.log_sigmoid(GATE_CAP * jnp.tanh(g_scr[rows, LANES:2 * LANES] / GATE_CAP))
        f_hi, f_mid, f_lo = _split3(logf)
        bcum = (_dot(tri, f_hi.astype(BF16)) + _dot(tri, f_mid.astype(BF16))
                + _dot(tri, f_lo.astype(BF16)))
        r = ii - bcum
        cmax = _prefix_max_rows(r)
        m_lane = mlane_scr[0:1, :]
        mx = jnp.maximum(cmax, m_lane)
        b_last_l = bcum[L - 1:L, :]
        m_new_l = jnp.maximum(b_last_l + m_lane, b_last_l + cmax[L - 1:L, :])
        mlane_scr[...] = jnp.broadcast_to(m_new_l, mlane_scr.shape)

        lane = lax.broadcasted_iota(jnp.int32, (L, LANES), 1)
        nm = _split3(-LOG2_E * mx)
        nb = _split3(-LOG2_E * bcum)
        t = jnp.where(lane < _T_END, 1.0, 0.0)
        for part in (2, 1, 0):
            t = jnp.where(lane < _T_NEG_B + (part + 1) * M_HEADS, nb[part], t)
        for part in (2, 1, 0):
            t = jnp.where(lane < _T_NEG_M + (part + 1) * M_HEADS, nm[part], t)

        r_row = r.T[0:M_HEADS, :]
        bc_row = bcum.T[0:M_HEADS, :]
        rmax = jnp.max(r_row, axis=1, keepdims=True)
        wgt_row = jnp.exp(r_row - rmax)
        b_last = bc_row[:, L - 1:L]
        m_st = msub_scr[...]
        m_loc = b_last + rmax
        m_new = jnp.maximum(b_last + m_st, m_loc)
        s_old = jnp.exp(b_last + m_st - m_new)
        s_new = jnp.exp(m_loc - m_new)
        msub_scr[...] = m_new

        r_parts = _split3(LOG2_E * r_row)
        m_parts = _split3(LOG2_E * m_st)
        data = []
        for part in range(3):
            blocks = []
            for hd in range(M_HEADS):
                blocks += [jnp.where(sub8 == hd, r_parts[part], 0.0),
                           jnp.where(sub8 == hd, m_parts[part], 0.0), zeros_r]
            data.append(jnp.concatenate(blocks, axis=1))
        data.append(jnp.zeros((M_HEADS, M_ARG_COLS), F32))
        rhs_scr[c, _T_ONES:_T_ONES + _R_DATA_ROWS, :] = (
            jnp.concatenate(data, axis=0).astype(BF16))
        e_scr[c] = jnp.exp2(_dot(t.astype(BF16), rhs_scr[c]))

        for pair in range(M_HEADS // 2):
            pc = pair * LANES
            q2f = p_scr[rows, pc:pc + LANES]
            kT = (p_scr[rows, M_QK + pc:M_QK + pc + LANES] * (M_DK ** -0.5)).T
            kt_scr[c, pair] = kT
            bd = jnp.concatenate(
                [jnp.concatenate([kT[0:M_DK], zeros_k], axis=0),
                 jnp.concatenate([zeros_k, kT[M_DK:2 * M_DK]], axis=0)], axis=1)
            s_pair = _dot(q2f.astype(BF16), bd.astype(BF16))
            for half in range(2):
                hd = 2 * pair + half
                e0 = hd * M_ARG_BLOCK
                decay = jnp.where(causal, e_scr[c, :, e0:e0 + LANES], 0.0)
                scores = (s_pair[:, half * L:(half + 1) * L] * decay).astype(BF16)
                qi = (q2f * e_scr[c, :, e0 + LANES:e0 + 2 * LANES]).astype(BF16)
                a_scr[c, hd] = jnp.concatenate([scores, qi], axis=1)

        for hd in range(M_HEADS):
            pair, half = divmod(hd, 2)
            e0 = hd * M_ARG_BLOCK
            vc = 2 * M_QK + hd * M_DV
            vext = jnp.concatenate([p_scr[rows, vc:vc + M_DV].astype(BF16), ones_blk], axis=1)
            s_h = s_scr[pair, half * M_DK:(half + 1) * M_DK, :]
            sz = jnp.concatenate([s_h, zeros_s] if half == 0 else [zeros_s, s_h], axis=0)
            nd = _dot(a_scr[c, hd], jnp.concatenate([vext, sz.astype(BF16)], axis=0))
            hh_scr[c, :, hd * M_DV:(hd + 1) * M_DV] = nd[:, :M_DV] / jnp.maximum(
                jnp.abs(nd[:, M_DV:]), e_scr[c, :, e0 + 2 * LANES:e0 + 3 * LANES])

        for hd in range(M_HEADS):
            hc = hd * M_DV
            oc = 2 * M_QK + D_MODEL + hc
            hh = hh_scr[c, :, hc:hc + M_DV]
            hh = hh * lax.rsqrt(jnp.mean(hh * hh, axis=-1, keepdims=True) + EPS)
            hh = hh * hg_ref[:, hc:hc + M_DV]
            y_scr[rows, hc:hc + M_DV] = (jax.nn.sigmoid(p_scr[rows, oc:oc + M_DV]) * hh).astype(BF16)

        for hd in range(M_HEADS):
            pair, half = divmod(hd, 2)
            vc = 2 * M_QK + hd * M_DV
            vext = jnp.concatenate([p_scr[rows, vc:vc + M_DV].astype(BF16), ones_blk], axis=1)
            kT_h = kt_scr[c, pair, half * M_DK:(half + 1) * M_DK, :]
            upd = _dot((kT_h * wgt_row[hd:hd + 1, :]).astype(BF16), vext)
            s_h = s_scr[pair, half * M_DK:(half + 1) * M_DK, :]
            s_scr[pair, half * M_DK:(half + 1) * M_DK, :] = (
                jnp.concatenate([s_old[hd:hd + 1, :], s_old[hd:hd + 1, :]], axis=1) * s_h
                + jnp.concatenate([s_new[hd:hd + 1, :], s_new[hd:hd + 1, :]], axis=1) * upd)

    x = x_ref[...]
    mod = mod_ref[...]
    ng = ng_ref[...]
    h = (_rms(x, ng[0:1] * (1.0 + mod[1:2])) + mod[0:1]).astype(BF16)
    p_scr[...] = _wdot(h, win_ref[...])
    g_scr[...] = _wdot(h, wg_ref[...]) + bg_ref[...]
    for c in range(ts // L):
        rhs_scr[c] = sel_ref[...]
    for c in range(ts // L):
        chunk(c)
    y = _wdot(y_scr[...], wout_ref[...])
    o_ref[...] = x + _rms(y, mod[2:3] * ng[1:2])


def _gate_copies(a, lo):
    g = jnp.tile(a[..., lo:lo + M_HEADS], M_GATE_COPIES)
    return jnp.pad(g, [(0, 0)] * (g.ndim - 1) + [(0, LANES - M_GATE_COPIES * M_HEADS)])


def _mlstm(x2, mods, norm_g, w_in, w_g, b_g, hnorm_g, w_out, layer, j, bsz, seq):
    ts = TOKEN_TILE
    per_seq = seq // ts
    sel = jnp.asarray(_selector_const(), BF16)
    chunks = ts // M_CHUNK
    return pl.pallas_call(
        _mlstm_kernel,
        grid=(bsz, per_seq),
        in_specs=[
            pl.BlockSpec((ts, D_MODEL), lambda b, t: (b * per_seq + t, 0)),
            pl.BlockSpec((None, None, 6, D_MODEL), lambda b, t: (layer, b, 0, 0)),
            _layer((4, D_MODEL), layer),
            _layer((D_MODEL, M_QKVO), j),
            _layer((D_MODEL, 2 * LANES), j),
            _layer((1, 2 * LANES), j),
            _layer((1, D_MODEL), j),
            _resident((LANES, M_ARG_COLS)),
            _layer((D_MODEL, D_MODEL), j),
        ],
        out_specs=pl.BlockSpec((ts, D_MODEL), lambda b, t: (b * per_seq + t, 0)),
        out_shape=jax.ShapeDtypeStruct(x2.shape, F32),
        scratch_shapes=[
            pltpu.VMEM((ts, M_QKVO), F32),
            pltpu.VMEM((ts, 2 * LANES), F32),
            pltpu.VMEM((ts, D_MODEL), BF16),
            pltpu.VMEM((M_HEADS // 2, 2 * M_DK, 2 * M_DV), F32),
            pltpu.VMEM((M_HEADS, LANES), F32),
            pltpu.VMEM((M_HEADS, LANES), F32),
            pltpu.VMEM((chunks, LANES, M_ARG_COLS), BF16),
            pltpu.VMEM((chunks, M_CHUNK, M_ARG_COLS), F32),
            pltpu.VMEM((chunks, M_HEADS, M_CHUNK, 2 * LANES), BF16),
            pltpu.VMEM((chunks, M_HEADS // 2, LANES, M_CHUNK), F32),
            pltpu.VMEM((chunks, M_CHUNK, D_MODEL), F32),
        ],
        compiler_params=_params(("arbitrary", "arbitrary")),
        name="mlstm_mixer",
    )(x2, mods, norm_g, w_in, w_g, b_g, hnorm_g, sel, w_out)


def kernel(x, c, norm_g, ada_w, ada_b, ffn_w1, ffn_w2, a_w_in, a_b_if, a_hnorm_g, a_w_out,
           b_w_in, b_b_in, b_ln_g, b_ln_b, b_ws, b_bs, b_w_out):
    bsz, seq, d = x.shape
    assert d == D_MODEL and seq % TOKEN_TILE == 0 and norm_g.shape[0] == DEPTH
    mods = _ada(c, ada_w, ada_b).reshape(DEPTH, bsz, 6, D_MODEL)
    a_w_g = jnp.concatenate([_gate_copies(a_w_in, M_QKVO), _gate_copies(a_w_in, M_QKVO + M_HEADS)],
                            axis=-1)
    a_b_g = jnp.concatenate([_gate_copies(a_b_if, 0), _gate_copies(a_b_if, M_HEADS)],
                            axis=-1)[:, None, :]
    a_hnorm_g = a_hnorm_g[:, None, :]
    b_b_in, b_ln_g, b_ln_b = b_b_in[:, None, :], b_ln_g[:, None, :], b_ln_b[:, None, :]
    b_bsb = jnp.broadcast_to(0.5 * b_bs[..., None], b_bs.shape + (LANES,))
    x2 = x.reshape(bsz * seq, D_MODEL)
    for i in range(DEPTH):
        j = i // 2
        if i % 2 == 0:
            x2 = _mlstm(x2, mods, norm_g, a_w_in, a_w_g, a_b_g, a_hnorm_g, a_w_out,
                        i, j, bsz, seq)
        else:
            x2 = _sgu(x2, mods, norm_g, b_w_in, b_b_in, b_ln_g, b_ln_b, b_ws, b_bsb, b_w_out,
                      i, j, seq)
        x2 = _ffn(x2, mods, norm_g, ffn_w1, ffn_w2, i, seq)
    return x2.reshape(bsz, seq, D_MODEL)
```

```python
import functools

import numpy as np

import jax
import jax.numpy as jnp
from jax import lax
from jax.experimental import pallas as pl
from jax.experimental.pallas import tpu as pltpu

F32 = jnp.float32
BF16 = jnp.bfloat16

D_MODEL = 1024
DEPTH = 4
EPS = 1e-6

M_HEADS = 8
M_DV = 128
M_DK = 64
M_CHUNK = 128
GATE_CAP = 15.0
LOG2_E = 1.4426950408889634
M_QK = M_HEADS * M_DK
M_QKVO = 2 * M_QK + 2 * D_MODEL

G_CHUNK = 128
G_HALF = 2048
G_GROUP = 128
G_GROUPS = 16

FFN_HIDDEN = 4096

LANES = 128
MXU_COLS = 256
VMEM_LIMIT_BYTES = 60 * 1024 * 1024

TOKEN_TILE = 512
FFN_HIDDEN_TILE = 1024


def _dot(a, b):
    return jnp.dot(a, b, preferred_element_type=F32)


def _wdot(a, w):
    return _dot(a, w.astype(BF16))


def _rms(x, g):
    return x * lax.rsqrt(jnp.mean(x * x, axis=-1, keepdims=True) + EPS) * g


def _zero_words(v):
    u = pltpu.bitcast(v, jnp.uint32)
    acc = u[0:8]
    for r in range(8, u.shape[0], 8):
        acc = acc | u[r:r + 8]
    return (acc >> 16) >> 16


def _after(w, zero_words):
    rows = 8 * 4 // w.dtype.itemsize
    k = zero_words.shape[1]
    top = pltpu.bitcast(pltpu.bitcast(w[0:rows, 0:k], jnp.uint32) | zero_words, w.dtype)
    if k < w.shape[1]:
        top = jnp.concatenate([top, w[0:rows, k:]], axis=1)
    return jnp.concatenate([top, w[rows:]], axis=0)


def _resident(shape):
    zeros = (0,) * len(shape)
    return pl.BlockSpec(shape, lambda *_: zeros, pipeline_mode=pl.Buffered(1))


def _layer(shape, layer):
    index = (layer,) + (0,) * len(shape)
    return pl.BlockSpec((None,) + tuple(shape), lambda *_: index, pipeline_mode=pl.Buffered(1))


def _params(semantics, flags=None):
    return pltpu.CompilerParams(dimension_semantics=semantics,
                                vmem_limit_bytes=VMEM_LIMIT_BYTES, flags=flags)


def _ada_kernel(c_ref, w_ref, b_ref, o_ref):
    c = c_ref[...]
    ca = (c * jax.nn.sigmoid(c)).astype(BF16)
    o_ref[...] = _wdot(ca, w_ref[...]) + b_ref[...]


def _ada(c, ada_w, ada_b):
    bsz = c.shape[0]
    blk = 2 * D_MODEL
    return pl.pallas_call(
        _ada_kernel,
        grid=(DEPTH, 6 * D_MODEL // blk),
        in_specs=[
            pl.BlockSpec((bsz, D_MODEL), lambda i, j: (0, 0)),
            pl.BlockSpec((None, D_MODEL, blk), lambda i, j: (i, 0, j)),
            pl.BlockSpec((None, 1, blk), lambda i, j: (i, 0, j)),
        ],
        out_specs=pl.BlockSpec((None, bsz, blk), lambda i, j: (i, 0, j)),
        out_shape=jax.ShapeDtypeStruct((DEPTH, bsz, 6 * D_MODEL), F32),
        compiler_params=_params(("arbitrary", "arbitrary")),
        name="ada_mod",
    )(c, ada_w, ada_b.reshape(DEPTH, 1, 6 * D_MODEL))


_MIXER_ROWS = (0, 0, 1, 1, 2)
_FFN_ROWS = (2, 3, 4, 3, 5)


def _pipelined(n, piece, tiles, refs, rows, core, start_fetch):
    xn_ref, xp_ref, modn_ref, modp_ref, ng_ref, o_ref, h_scr, y_scr = refs
    g_in, shift, scale, g_out, gate = rows
    i = pl.program_id(0)
    tm = xn_ref.shape[0]
    ng = ng_ref[...]

    def normalise(x_ref, mod_ref, slot, r0):
        mod = mod_ref[...]
        gain = ng[g_in:g_in + 1] * (1.0 + mod[scale:scale + 1])
        hn = (_rms(x_ref[r0:r0 + piece, :], gain) + mod[shift:shift + 1]).astype(BF16)
        h_scr[slot, r0:r0 + piece, :] = hn
        return hn

    def finish(slot, r0):
        gain = modp_ref[gate:gate + 1, :] * ng[g_out:g_out + 1]
        o = xp_ref[r0:r0 + piece, :] + _rms(y_scr[slot, r0:r0 + piece, :], gain)
        o_ref[r0:r0 + piece, :] = o
        return o

    @pl.when(i == 0)
    def _():
        start_fetch()
        for r0 in range(0, tm, piece):
            normalise(xp_ref, modp_ref, 0, r0)
        y_scr[1] = jnp.zeros(y_scr.shape[1:], F32)

    def step(slot, first):
        work = []
        for r0 in range(0, tm, piece):
            work.append(functools.partial(finish, 1 - slot, r0))
            work.append(functools.partial(normalise, xn_ref, modn_ref, 1 - slot, r0))

        left = [tiles]

        def anchored(w):
            take = -(-len(work) // left[0])
            left[0] -= 1
            for _ in range(take):
                w = _after(w, _zero_words(work.pop(0)())[:, :w.shape[1]])
            return w

        core(slot, anchored, first)
        assert not work and left[0] == 0, "core must pass exactly `tiles` weight tiles"

    even = lax.rem(i, 2) == 0
    pl.when(i == 0)(lambda: step(0, True))
    pl.when((i > 0) & (i < n) & even)(lambda: step(0, False))
    pl.when((i < n) & jnp.logical_not(even))(lambda: step(1, False))

    @pl.when(i == n)
    def _():
        for r0 in range(0, tm, piece):
            finish((n - 1) % 2, r0)


def _pipelined_specs(n, tm, per_seq, layer):
    nxt = lambda i: jnp.minimum(i + 1, n - 1)
    prv = lambda i: jnp.maximum(i - 1, 0)
    in_specs = [
        pl.BlockSpec((tm, D_MODEL), lambda i: (nxt(i), 0)),
        pl.BlockSpec((tm, D_MODEL), lambda i: (prv(i), 0)),
        pl.BlockSpec((None, None, 6, D_MODEL), lambda i: (layer, nxt(i) // per_seq, 0, 0)),
        pl.BlockSpec((None, None, 6, D_MODEL), lambda i: (layer, prv(i) // per_seq, 0, 0)),
        _layer((4, D_MODEL), layer),
    ]
    return in_specs, pl.BlockSpec((tm, D_MODEL), lambda i: (prv(i), 0))


def _pipelined_scratch(tm):
    return [pltpu.VMEM((2, tm, D_MODEL), BF16),
            pltpu.VMEM((2, tm, D_MODEL), F32)]


def _ffn_kernel(xn_ref, xp_ref, modn_ref, modp_ref, ng_ref, w1_hbm, w2_hbm, o_ref,
                h_scr, y_scr, w1_ref, w2_ref, sem, *, n, layer):
    chunks = list(range(0, FFN_HIDDEN, FFN_HIDDEN_TILE))

    def fetch(which, k):
        k0 = chunks[k]
        if which == 0:
            src = w1_hbm.at[layer, :, k0:k0 + FFN_HIDDEN_TILE]
            dst = w1_ref.at[:, k0:k0 + FFN_HIDDEN_TILE]
        else:
            src = w2_hbm.at[layer, k0:k0 + FFN_HIDDEN_TILE, :]
            dst = w2_ref.at[k0:k0 + FFN_HIDDEN_TILE, :]
        return pltpu.make_async_copy(src, dst, sem.at[which, k])

    def start_fetch():
        for k in range(len(chunks)):
            fetch(0, k).start()
            fetch(1, k).start()

    def core(slot, anchored, first):
        h = h_scr[slot]
        y = [None] * (D_MODEL // MXU_COLS)
        for k, k0 in enumerate(chunks):
            if first:
                fetch(0, k).wait()
            a = []
            for c0 in range(k0, k0 + FFN_HIDDEN_TILE, MXU_COLS):
                aj = _wdot(h, anchored(w1_ref[:, c0:c0 + MXU_COLS]))
                a.append(jnp.square(jnp.maximum(aj, 0.0)).astype(BF16))
            a = jnp.concatenate(a, axis=1)
            if first:
                fetch(1, k).wait()
            for j, c0 in enumerate(range(0, D_MODEL, MXU_COLS)):
                yk = _wdot(a, anchored(w2_ref[k0:k0 + FFN_HIDDEN_TILE, c0:c0 + MXU_COLS]))
                y[j] = yk if y[j] is None else y[j] + yk
        y_scr[slot] = jnp.concatenate(y, axis=1)

    tiles = (FFN_HIDDEN // MXU_COLS) + (FFN_HIDDEN // FFN_HIDDEN_TILE) * (D_MODEL // MXU_COLS)
    _pipelined(n, 32, tiles, (xn_ref, xp_ref, modn_ref, modp_ref, ng_ref, o_ref, h_scr, y_scr),
               _FFN_ROWS, core, start_fetch)


def _ffn(x2, mods, norm_g, w1, w2, layer, seq):
    tokens = x2.shape[0]
    tm = TOKEN_TILE
    per_seq = seq // tm
    n = tokens // tm
    in_specs, out_spec = _pipelined_specs(n, tm, per_seq, layer)
    return pl.pallas_call(
        functools.partial(_ffn_kernel, n=n, layer=layer),
        grid=(n + 1,),
        in_specs=in_specs + [
            pl.BlockSpec(memory_space=pl.ANY),
            pl.BlockSpec(memory_space=pl.ANY),
        ],
        out_specs=out_spec,
        out_shape=jax.ShapeDtypeStruct(x2.shape, F32),
        scratch_shapes=_pipelined_scratch(tm) + [
            pltpu.VMEM((D_MODEL, FFN_HIDDEN), F32),
            pltpu.VMEM((FFN_HIDDEN, D_MODEL), F32),
            pltpu.SemaphoreType.DMA((2, FFN_HIDDEN // FFN_HIDDEN_TILE)),
        ],
        compiler_params=_params(("arbitrary",)),
        name="ffn_relu2",
    )(x2, x2, mods, mods, norm_g, w1, w2)


def _gelu_x2(z):
    return z * (1.0 + lax.erf(z * (2.0 ** -0.5)))


def _sgu_kernel(x_ref, mod_ref, ng_ref, win_ref, bin_ref, lng_ref, lnb_ref,
                ws_ref, bsb_ref, wout_ref, o_ref, y_scr):
    x = x_ref[...]
    mod = mod_ref[...]
    ng = ng_ref[...]
    tm = x.shape[0]
    h = (_rms(x, ng[0:1] * (1.0 + mod[1:2])) + mod[0:1]).astype(BF16)
    zv = _gelu_x2(_wdot(h, win_ref[:, G_HALF:]) + bin_ref[:, G_HALF:])
    mu = jnp.mean(zv, axis=-1, keepdims=True)
    zc = zv - mu
    var = jnp.mean(zc * zc, axis=-1, keepdims=True)
    vn = (zc * lax.rsqrt(var + 4.0 * EPS) * lng_ref[...] + lnb_ref[...]).astype(BF16)
    zu = _gelu_x2(_wdot(h, win_ref[:, :G_HALF]) + bin_ref[:, :G_HALF])

    row = lax.broadcasted_iota(jnp.int32, (G_CHUNK, G_CHUNK), 0)
    col = lax.broadcasted_iota(jnp.int32, (G_CHUNK, G_CHUNK), 1)
    causal = col <= row
    for g in range(G_GROUPS):
        c0 = g * G_GROUP
        wm = jnp.where(causal, 0.5 * ws_ref[g], 0.0).astype(BF16)
        bsg = bsb_ref[g]
        for r0 in range(0, tm, 2 * G_CHUNK):
            r1 = r0 + G_CHUNK
            r2 = r1 + G_CHUNK
            rhs = jnp.concatenate([vn[r0:r1, c0:c0 + G_GROUP],
                                   vn[r1:r2, c0:c0 + G_GROUP]], axis=1)
            sv = _dot(wm, rhs)
            y_scr[r0:r1, c0:c0 + G_GROUP] = (
                zu[r0:r1, c0:c0 + G_GROUP] * (sv[:, :G_GROUP] + bsg)).astype(BF16)
            y_scr[r1:r2, c0:c0 + G_GROUP] = (
                zu[r1:r2, c0:c0 + G_GROUP] * (sv[:, G_GROUP:] + bsg)).astype(BF16)
    y = _wdot(y_scr[...], wout_ref[...])
    o_ref[...] = x + _rms(y, mod[2:3] * ng[1:2])


def _sgu(x2, mods, norm_g, w_in, b_in, ln_g, ln_b, ws, bsb, w_out, layer, j, seq):
    tokens = x2.shape[0]
    tm = TOKEN_TILE
    per_seq = seq // tm
    return pl.pallas_call(
        _sgu_kernel,
        grid=(tokens // tm,),
        in_specs=[
            pl.BlockSpec((tm, D_MODEL), lambda i: (i, 0)),
            pl.BlockSpec((None, None, 6, D_MODEL), lambda i: (layer, i // per_seq, 0, 0)),
            _layer((4, D_MODEL), layer),
            _layer((D_MODEL, 2 * G_HALF), j),
            _layer((1, 2 * G_HALF), j),
            _layer((1, G_HALF), j),
            _layer((1, G_HALF), j),
            _layer((G_GROUPS, G_CHUNK, G_CHUNK), j),
            _layer((G_GROUPS, G_CHUNK, LANES), j),
            _layer((G_HALF, D_MODEL), j),
        ],
        out_specs=pl.BlockSpec((tm, D_MODEL), lambda i: (i, 0)),
        out_shape=jax.ShapeDtypeStruct(x2.shape, F32),
        scratch_shapes=[pltpu.VMEM((tm, G_HALF), BF16)],
        compiler_params=_params(("arbitrary",)),
        name="sgu_mixer",
    )(x2, mods, norm_g, w_in, b_in, ln_g, ln_b, ws, bsb, w_out)


M_ARG_BLOCK = 3 * LANES
M_ARG_COLS = M_HEADS * M_ARG_BLOCK
_T_NEG_M = 0
_T_NEG_B = 3 * M_HEADS
_T_ONES = 6 * M_HEADS
_T_END = 9 * M_HEADS
_R_DATA_ROWS = 32


def _selector_const():
    r = np.zeros((LANES, M_ARG_COLS), np.float32)
    for hd in range(M_HEADS):
        c0 = hd * M_ARG_BLOCK
        for part in range(3):
            r[_T_NEG_M + part * M_HEADS + hd, c0:c0 + M_ARG_BLOCK] = 1.0
            r[_T_NEG_B + part * M_HEADS + hd, c0 + 2 * LANES:c0 + 3 * LANES] = 1.0
    return r


def _split3(v):
    hi = v.astype(BF16).astype(F32)
    r1 = v - hi
    mid = r1.astype(BF16).astype(F32)
    lo = (r1 - mid).astype(BF16).astype(F32)
    return hi, mid, lo


def _segment_scan_lanes(v, op, identity, seg):
    lane = lax.rem(lax.broadcasted_iota(jnp.int32, v.shape, v.ndim - 1), seg)
    s = 1
    while s < seg:
        v = op(v, jnp.where(lane >= s, pltpu.roll(v, s, axis=v.ndim - 1), identity))
        s *= 2
    return v


def _mlstm_kernel(x_ref, mod_ref, ng_ref, win_ref, wg_ref, bg_ref, hg_ref, sel_ref,
                  wout_ref, o_ref, p_scr, y_scr, s_scr, msub_scr,
                  rhs_scr, e_scr, a_scr, kt_scr, hh_scr):
    L = M_CHUNK
    ts = x_ref.shape[0]

    @pl.when(pl.program_id(1) == 0)
    def _():
        s_scr[...] = jnp.zeros(s_scr.shape, F32)
        msub_scr[...] = jnp.zeros(msub_scr.shape, F32)

    row = lax.broadcasted_iota(jnp.int32, (L, L), 0)
    col = lax.broadcasted_iota(jnp.int32, (L, L), 1)
    causal = col <= row
    sub8 = lax.broadcasted_iota(jnp.int32, (M_HEADS, LANES), 0)
    ones_blk = jnp.ones((L, M_DV), BF16)
    zeros_k = jnp.zeros((M_DK, L), F32)
    zeros_s = jnp.zeros((M_DK, 2 * M_DV), F32)
    zeros_r = jnp.zeros((M_HEADS, LANES), F32)

    def chunk(c):
        rows = slice(c * L, (c + 1) * L)

        bc_row = bc_all[:, rows]
        r_row = r_all[:, rows]
        cmax = cmax_all[:, rows]
        m_st = msub_scr[...]
        mx = jnp.maximum(cmax, m_st)
        rmax = cmax[:, L - 1:L]
        wgt_row = jnp.exp(r_row - rmax)
        b_last = bc_row[:, L - 1:L]
        m_loc = b_last + rmax
        m_new = jnp.maximum(b_last + m_st, m_loc)
        s_old = jnp.exp(b_last + m_st - m_new)
        s_new = jnp.exp(m_loc - m_new)
        msub_scr[...] = m_new

        pieces = (list(_split3(-LOG2_E * mx)) + list(_split3(-LOG2_E * bc_row))
                  + [jnp.ones((_T_END - _T_ONES, L), F32), jnp.zeros((LANES - _T_END, L), F32)])
        t = jnp.concatenate(pieces, axis=0).T

        r_parts = _split3(LOG2_E * r_row)
        m_parts = _split3(LOG2_E * m_st)
        data = []
        for part in range(3):
            blocks = []
            for hd in range(M_HEADS):
                blocks += [jnp.where(sub8 == hd, r_parts[part], 0.0),
                           jnp.where(sub8 == hd, m_parts[part], 0.0), zeros_r]
            data.append(jnp.concatenate(blocks, axis=1))
        data.append(jnp.zeros((M_HEADS, M_ARG_COLS), F32))
        rhs_scr[c, _T_ONES:_T_ONES + _R_DATA_ROWS, :] = (
            jnp.concatenate(data, axis=0).astype(BF16))
        e_scr[c] = jnp.exp2(_dot(t.astype(BF16), rhs_scr[c]))

        for pair in range(M_HEADS // 2):
            pc = pair * LANES
            q2f = p_scr[rows, pc:pc + LANES]
            kT = (p_scr[rows, M_QK + pc:M_QK + pc + LANES] * (M_DK ** -0.5)).T
            kt_scr[c, pair] = kT
            bd = jnp.concatenate(
                [jnp.concatenate([kT[0:M_DK], zeros_k], axis=0),
                 jnp.concatenate([zeros_k, kT[M_DK:2 * M_DK]], axis=0)], axis=1)
            s_pair = _dot(q2f.astype(BF16), bd.astype(BF16))
            for half in range(2):
                hd = 2 * pair + half
                e0 = hd * M_ARG_BLOCK
                decay = jnp.where(causal, e_scr[c, :, e0:e0 + LANES], 0.0)
                scores = (s_pair[:, half * L:(half + 1) * L] * decay).astype(BF16)
                qi = (q2f * e_scr[c, :, e0 + LANES:e0 + 2 * LANES]).astype(BF16)
                a_scr[c, hd] = jnp.concatenate([scores, qi], axis=1)

        for hd in range(M_HEADS):
            pair, half = divmod(hd, 2)
            e0 = hd * M_ARG_BLOCK
            vc = 2 * M_QK + hd * M_DV
            vext = jnp.concatenate([p_scr[rows, vc:vc + M_DV].astype(BF16), ones_blk], axis=1)
            s_h = s_scr[pair, half * M_DK:(half + 1) * M_DK, :]
            sz = jnp.concatenate([s_h, zeros_s] if half == 0 else [zeros_s, s_h], axis=0)
            nd = _dot(a_scr[c, hd], jnp.concatenate([vext, sz.astype(BF16)], axis=0))
            hh_scr[c, :, hd * M_DV:(hd + 1) * M_DV] = nd[:, :M_DV] / jnp.maximum(
                jnp.abs(nd[:, M_DV:]), e_scr[c, :, e0 + 2 * LANES:e0 + 3 * LANES])

        for hd in range(M_HEADS):
            hc = hd * M_DV
            oc = 2 * M_QK + D_MODEL + hc
            hh = hh_scr[c, :, hc:hc + M_DV]
            hh = hh * lax.rsqrt(jnp.mean(hh * hh, axis=-1, keepdims=True) + EPS)
            hh = hh * hg_ref[:, hc:hc + M_DV]
            y_scr[rows, hc:hc + M_DV] = (jax.nn.sigmoid(p_scr[rows, oc:oc + M_DV]) * hh).astype(BF16)

        for hd in range(M_HEADS):
            pair, half = divmod(hd, 2)
            vc = 2 * M_QK + hd * M_DV
            vext = jnp.concatenate([p_scr[rows, vc:vc + M_DV].astype(BF16), ones_blk], axis=1)
            kT_h = kt_scr[c, pair, half * M_DK:(half + 1) * M_DK, :]
            upd = _dot((kT_h * wgt_row[hd:hd + 1, :]).astype(BF16), vext)
            s_h = s_scr[pair, half * M_DK:(half + 1) * M_DK, :]
            s_scr[pair, half * M_DK:(half + 1) * M_DK, :] = (
                jnp.concatenate([s_old[hd:hd + 1, :], s_old[hd:hd + 1, :]], axis=1) * s_h
                + jnp.concatenate([s_new[hd:hd + 1, :], s_new[hd:hd + 1, :]], axis=1) * upd)

    x = x_ref[...]
    mod = mod_ref[...]
    ng = ng_ref[...]
    h = (_rms(x, ng[0:1] * (1.0 + mod[1:2])) + mod[0:1]).astype(BF16)
    g_t = (_wdot(h, wg_ref[...]) + bg_ref[...]).T
    ii_all = GATE_CAP * jnp.tanh(g_t[0:M_HEADS] / GATE_CAP)
    logf = jax.nn.log_sigmoid(GATE_CAP * jnp.tanh(g_t[M_HEADS:2 * M_HEADS] / GATE_CAP))
    bc_all = _segment_scan_lanes(logf, jnp.add, 0.0, L)
    r_all = ii_all - bc_all
    cmax_all = _segment_scan_lanes(r_all, jnp.maximum, -jnp.inf, L)
    last = M_QKVO - MXU_COLS
    p_scr[:, :last] = _wdot(h, win_ref[:, :last])
    scanned = _zero_words(cmax_all)
    scanned = scanned[:, :MXU_COLS] | scanned[:, MXU_COLS:]
    p_scr[:, last:] = _wdot(h, _after(win_ref[:, last:], scanned))
    for c in range(ts // L):
        rhs_scr[c] = sel_ref[...]
    for c in range(ts // L):
        chunk(c)
    y = _wdot(y_scr[...], wout_ref[...])
    o_ref[...] = x + _rms(y, mod[2:3] * ng[1:2])


def _pad_lanes(a):
    return jnp.pad(a, [(0, 0)] * (a.ndim - 1) + [(0, LANES - a.shape[-1])])


def _mlstm(x2, mods, norm_g, w_in, w_g, b_g, hnorm_g, w_out, layer, j, bsz, seq):
    ts = TOKEN_TILE
    per_seq = seq // ts
    sel = jnp.asarray(_selector_const(), BF16)
    chunks = ts // M_CHUNK
    return pl.pallas_call(
        _mlstm_kernel,
        grid=(bsz, per_seq),
        in_specs=[
            pl.BlockSpec((ts, D_MODEL), lambda b, t: (b * per_seq + t, 0)),
            pl.BlockSpec((None, None, 6, D_MODEL), lambda b, t: (layer, b, 0, 0)),
            _layer((4, D_MODEL), layer),
            _layer((D_MODEL, M_QKVO), j),
            _layer((D_MODEL, LANES), j),
            _layer((1, LANES), j),
            _layer((1, D_MODEL), j),
            _resident((LANES, M_ARG_COLS)),
            _layer((D_MODEL, D_MODEL), j),
        ],
        out_specs=pl.BlockSpec((ts, D_MODEL), lambda b, t: (b * per_seq + t, 0)),
        out_shape=jax.ShapeDtypeStruct(x2.shape, F32),
        scratch_shapes=[
            pltpu.VMEM((ts, M_QKVO), F32),
            pltpu.VMEM((ts, D_MODEL), BF16),
            pltpu.VMEM((M_HEADS // 2, 2 * M_DK, 2 * M_DV), F32),
            pltpu.VMEM((M_HEADS, LANES), F32),
            pltpu.VMEM((chunks, LANES, M_ARG_COLS), BF16),
            pltpu.VMEM((chunks, M_CHUNK, M_ARG_COLS), F32),
            pltpu.VMEM((chunks, M_HEADS, M_CHUNK, 2 * LANES), BF16),
            pltpu.VMEM((chunks, M_HEADS // 2, LANES, M_CHUNK), F32),
            pltpu.VMEM((chunks, M_CHUNK, D_MODEL), F32),
        ],
        compiler_params=_params(("arbitrary", "arbitrary")),
        name="mlstm_mixer",
    )(x2, mods, norm_g, w_in, w_g, b_g, hnorm_g, sel, w_out)


def kernel(x, c, norm_g, ada_w, ada_b, ffn_w1, ffn_w2, a_w_in, a_b_if, a_hnorm_g, a_w_out,
           b_w_in, b_b_in, b_ln_g, b_ln_b, b_ws, b_bs, b_w_out):
    bsz, seq, d = x.shape
    assert d == D_MODEL and seq % TOKEN_TILE == 0 and norm_g.shape[0] == DEPTH
    mods = _ada(c, ada_w, ada_b).reshape(DEPTH, bsz, 6, D_MODEL)
    a_w_g = _pad_lanes(a_w_in[..., M_QKVO:])
    a_b_g = _pad_lanes(a_b_if)[:, None, :]
    a_hnorm_g = a_hnorm_g[:, None, :]
    b_b_in, b_ln_g, b_ln_b = b_b_in[:, None, :], b_ln_g[:, None, :], b_ln_b[:, None, :]
    b_bsb = jnp.broadcast_to(0.5 * b_bs[..., None], b_bs.shape + (LANES,))
    x2 = x.reshape(bsz * seq, D_MODEL)
    for i in range(DEPTH):
        j = i // 2
        if i % 2 == 0:
            x2 = _mlstm(x2, mods, norm_g, a_w_in, a_w_g, a_b_g, a_hnorm_g, a_w_out,
                        i, j, bsz, seq)
        else:
            x2 = _sgu(x2, mods, norm_g, b_w_in, b_b_in, b_ln_g, b_ln_b, b_ws, b_bsb, b_w_out,
                      i, j, seq)
        x2 = _ffn(x2, mods, norm_g, ffn_w1, ffn_w2, i, seq)
    return x2.reshape(bsz, seq, D_MODEL)
```

```python
import functools

import numpy as np

import jax
import jax.numpy as jnp
from jax import lax
from jax.experimental import pallas as pl
from jax.experimental.pallas import tpu as pltpu

F32 = jnp.float32
BF16 = jnp.bfloat16

D_MODEL = 1024
DEPTH = 4
EPS = 1e-6

M_HEADS = 8
M_DV = 128
M_DK = 64
M_CHUNK = 128
GATE_CAP = 15.0
LOG2_E = 1.4426950408889634
M_QK = M_HEADS * M_DK
M_QKVO = 2 * M_QK + 2 * D_MODEL

G_CHUNK = 128
G_HALF = 2048
G_GROUP = 128
G_GROUPS = 16

FFN_HIDDEN = 4096

LANES = 128
MXU_COLS = 256
VMEM_LIMIT_BYTES = 60 * 1024 * 1024

TOKEN_TILE = 512
FFN_HIDDEN_TILE = 1024


def _dot(a, b):
    return jnp.dot(a, b, preferred_element_type=F32)


def _wdot(a, w):
    return _dot(a, w.astype(BF16))


def _rms(x, g):
    return x * lax.rsqrt(jnp.mean(x * x, axis=-1, keepdims=True) + EPS) * g


def _zero_words(v):
    u = pltpu.bitcast(v, jnp.uint32)
    acc = u[0:8]
    for r in range(8, u.shape[0], 8):
        acc = acc | u[r:r + 8]
    return (acc >> 16) >> 16


def _after(w, zero_words):
    rows = 8 * 4 // w.dtype.itemsize
    k = zero_words.shape[1]
    top = pltpu.bitcast(pltpu.bitcast(w[0:rows, 0:k], jnp.uint32) | zero_words, w.dtype)
    if k < w.shape[1]:
        top = jnp.concatenate([top, w[0:rows, k:]], axis=1)
    return jnp.concatenate([top, w[rows:]], axis=0)


def _resident(shape):
    zeros = (0,) * len(shape)
    return pl.BlockSpec(shape, lambda *_: zeros, pipeline_mode=pl.Buffered(1))


def _layer(shape, layer):
    index = (layer,) + (0,) * len(shape)
    return pl.BlockSpec((None,) + tuple(shape), lambda *_: index, pipeline_mode=pl.Buffered(1))


def _params(semantics, flags=None):
    return pltpu.CompilerParams(dimension_semantics=semantics,
                                vmem_limit_bytes=VMEM_LIMIT_BYTES, flags=flags)


def _ada_kernel(c_ref, w_ref, b_ref, o_ref):
    c = c_ref[...]
    ca = (c * jax.nn.sigmoid(c)).astype(BF16)
    o_ref[...] = _wdot(ca, w_ref[...]) + b_ref[...]


def _ada(c, ada_w, ada_b):
    bsz = c.shape[0]
    blk = 2 * D_MODEL
    return pl.pallas_call(
        _ada_kernel,
        grid=(DEPTH, 6 * D_MODEL // blk),
        in_specs=[
            pl.BlockSpec((bsz, D_MODEL), lambda i, j: (0, 0)),
            pl.BlockSpec((None, D_MODEL, blk), lambda i, j: (i, 0, j)),
            pl.BlockSpec((None, 1, blk), lambda i, j: (i, 0, j)),
        ],
        out_specs=pl.BlockSpec((None, bsz, blk), lambda i, j: (i, 0, j)),
        out_shape=jax.ShapeDtypeStruct((DEPTH, bsz, 6 * D_MODEL), F32),
        compiler_params=_params(("arbitrary", "arbitrary")),
        name="ada_mod",
    )(c, ada_w, ada_b.reshape(DEPTH, 1, 6 * D_MODEL))


_MIXER_ROWS = (0, 0, 1, 1, 2)
_FFN_ROWS = (2, 3, 4, 3, 5)


def _pipelined(n, piece, tiles, refs, rows, core, start_fetch):
    xn_ref, xp_ref, modn_ref, modp_ref, ng_ref, o_ref, h_scr, y_scr = refs
    g_in, shift, scale, g_out, gate = rows
    i = pl.program_id(0)
    tm = xn_ref.shape[0]
    ng = ng_ref[...]

    def normalise(x_ref, mod_ref, slot, r0):
        mod = mod_ref[...]
        gain = ng[g_in:g_in + 1] * (1.0 + mod[scale:scale + 1])
        hn = (_rms(x_ref[r0:r0 + piece, :], gain) + mod[shift:shift + 1]).astype(BF16)
        h_scr[slot, r0:r0 + piece, :] = hn
        return hn

    def finish(slot, r0):
        gain = modp_ref[gate:gate + 1, :] * ng[g_out:g_out + 1]
        o = xp_ref[r0:r0 + piece, :] + _rms(y_scr[slot, r0:r0 + piece, :], gain)
        o_ref[r0:r0 + piece, :] = o
        return o

    @pl.when(i == 0)
    def _():
        start_fetch()
        for r0 in range(0, tm, piece):
            normalise(xp_ref, modp_ref, 0, r0)
        y_scr[1] = jnp.zeros(y_scr.shape[1:], F32)

    def step(slot, first):
        work = []
        for r0 in range(0, tm, piece):
            work.append(functools.partial(finish, 1 - slot, r0))
            work.append(functools.partial(normalise, xn_ref, modn_ref, 1 - slot, r0))

        left = [tiles]

        def anchored(w):
            take = -(-len(work) // left[0])
            left[0] -= 1
            for _ in range(take):
                w = _after(w, _zero_words(work.pop(0)())[:, :w.shape[1]])
            return w

        core(slot, anchored, first)
        assert not work and left[0] == 0, "core must pass exactly `tiles` weight tiles"

    even = lax.rem(i, 2) == 0
    pl.when(i == 0)(lambda: step(0, True))
    pl.when((i > 0) & (i < n) & even)(lambda: step(0, False))
    pl.when((i < n) & jnp.logical_not(even))(lambda: step(1, False))

    @pl.when(i == n)
    def _():
        for r0 in range(0, tm, piece):
            finish((n - 1) % 2, r0)


def _pipelined_specs(n, tm, per_seq, layer):
    nxt = lambda i: jnp.minimum(i + 1, n - 1)
    prv = lambda i: jnp.maximum(i - 1, 0)
    in_specs = [
        pl.BlockSpec((tm, D_MODEL), lambda i: (nxt(i), 0)),
        pl.BlockSpec((tm, D_MODEL), lambda i: (prv(i), 0)),
        pl.BlockSpec((None, None, 6, D_MODEL), lambda i: (layer, nxt(i) // per_seq, 0, 0)),
        pl.BlockSpec((None, None, 6, D_MODEL), lambda i: (layer, prv(i) // per_seq, 0, 0)),
        _layer((4, D_MODEL), layer),
    ]
    return in_specs, pl.BlockSpec((tm, D_MODEL), lambda i: (prv(i), 0))


def _pipelined_scratch(tm):
    return [pltpu.VMEM((2, tm, D_MODEL), BF16),
            pltpu.VMEM((2, tm, D_MODEL), F32)]


def _ffn_kernel(xn_ref, xp_ref, modn_ref, modp_ref, ng_ref, w1_hbm, w2_hbm, o_ref,
                h_scr, y_scr, w1_ref, w2_ref, sem, *, n, layer):
    chunks = list(range(0, FFN_HIDDEN, FFN_HIDDEN_TILE))

    def fetch(which, k):
        k0 = chunks[k]
        if which == 0:
            src = w1_hbm.at[layer, :, k0:k0 + FFN_HIDDEN_TILE]
            dst = w1_ref.at[:, k0:k0 + FFN_HIDDEN_TILE]
        else:
            src = w2_hbm.at[layer, k0:k0 + FFN_HIDDEN_TILE, :]
            dst = w2_ref.at[k0:k0 + FFN_HIDDEN_TILE, :]
        return pltpu.make_async_copy(src, dst, sem.at[which, k])

    def start_fetch():
        for k in range(len(chunks)):
            fetch(0, k).start()
            fetch(1, k).start()

    def core(slot, anchored, first):
        h = h_scr[slot]
        y = [None] * (D_MODEL // MXU_COLS)
        for k, k0 in enumerate(chunks):
            if first:
                fetch(0, k).wait()
            a = []
            for c0 in range(k0, k0 + FFN_HIDDEN_TILE, MXU_COLS):
                aj = _wdot(h, anchored(w1_ref[:, c0:c0 + MXU_COLS]))
                a.append(jnp.square(jnp.maximum(aj, 0.0)).astype(BF16))
            a = jnp.concatenate(a, axis=1)
            if first:
                fetch(1, k).wait()
            for j, c0 in enumerate(range(0, D_MODEL, MXU_COLS)):
                yk = _wdot(a, anchored(w2_ref[k0:k0 + FFN_HIDDEN_TILE, c0:c0 + MXU_COLS]))
                y[j] = yk if y[j] is None else y[j] + yk
        y_scr[slot] = jnp.concatenate(y, axis=1)

    tiles = (FFN_HIDDEN // MXU_COLS) + (FFN_HIDDEN // FFN_HIDDEN_TILE) * (D_MODEL // MXU_COLS)
    _pipelined(n, 32, tiles, (xn_ref, xp_ref, modn_ref, modp_ref, ng_ref, o_ref, h_scr, y_scr),
               _FFN_ROWS, core, start_fetch)


def _ffn(x2, mods, norm_g, w1, w2, layer, seq):
    tokens = x2.shape[0]
    tm = TOKEN_TILE
    per_seq = seq // tm
    n = tokens // tm
    in_specs, out_spec = _pipelined_specs(n, tm, per_seq, layer)
    return pl.pallas_call(
        functools.partial(_ffn_kernel, n=n, layer=layer),
        grid=(n + 1,),
        in_specs=in_specs + [
            pl.BlockSpec(memory_space=pl.ANY),
            pl.BlockSpec(memory_space=pl.ANY),
        ],
        out_specs=out_spec,
        out_shape=jax.ShapeDtypeStruct(x2.shape, F32),
        scratch_shapes=_pipelined_scratch(tm) + [
            pltpu.VMEM((D_MODEL, FFN_HIDDEN), F32),
            pltpu.VMEM((FFN_HIDDEN, D_MODEL), F32),
            pltpu.SemaphoreType.DMA((2, FFN_HIDDEN // FFN_HIDDEN_TILE)),
        ],
        compiler_params=_params(("arbitrary",)),
        name="ffn_relu2",
    )(x2, x2, mods, mods, norm_g, w1, w2)


def _gelu_x2(z):
    return z * (1.0 + lax.erf(z * (2.0 ** -0.5)))


def _sgu_kernel(x_ref, mod_ref, ng_ref, win_ref, bin_ref, lng_ref, lnb_ref,
                ws_ref, bsb_ref, wout_ref, o_ref, y_scr):
    x = x_ref[...]
    mod = mod_ref[...]
    ng = ng_ref[...]
    tm = x.shape[0]
    h = (_rms(x, ng[0:1] * (1.0 + mod[1:2])) + mod[0:1]).astype(BF16)
    zv = _gelu_x2(_wdot(h, win_ref[:, G_HALF:]) + bin_ref[:, G_HALF:])
    mu = jnp.mean(zv, axis=-1, keepdims=True)
    zc = zv - mu
    var = jnp.mean(zc * zc, axis=-1, keepdims=True)
    vn = (zc * lax.rsqrt(var + 4.0 * EPS) * lng_ref[...] + lnb_ref[...]).astype(BF16)
    zu = _gelu_x2(_wdot(h, win_ref[:, :G_HALF]) + bin_ref[:, :G_HALF])

    row = lax.broadcasted_iota(jnp.int32, (G_CHUNK, G_CHUNK), 0)
    col = lax.broadcasted_iota(jnp.int32, (G_CHUNK, G_CHUNK), 1)
    causal = col <= row
    for g in range(G_GROUPS):
        c0 = g * G_GROUP
        wm = jnp.where(causal, 0.5 * ws_ref[g], 0.0).astype(BF16)
        bsg = bsb_ref[g]
        for r0 in range(0, tm, 2 * G_CHUNK):
            r1 = r0 + G_CHUNK
            r2 = r1 + G_CHUNK
            rhs = jnp.concatenate([vn[r0:r1, c0:c0 + G_GROUP],
                                   vn[r1:r2, c0:c0 + G_GROUP]], axis=1)
            sv = _dot(wm, rhs)
            y_scr[r0:r1, c0:c0 + G_GROUP] = (
                zu[r0:r1, c0:c0 + G_GROUP] * (sv[:, :G_GROUP] + bsg)).astype(BF16)
            y_scr[r1:r2, c0:c0 + G_GROUP] = (
                zu[r1:r2, c0:c0 + G_GROUP] * (sv[:, G_GROUP:] + bsg)).astype(BF16)
    y = _wdot(y_scr[...], wout_ref[...])
    o_ref[...] = x + _rms(y, mod[2:3] * ng[1:2])


def _sgu(x2, mods, norm_g, w_in, b_in, ln_g, ln_b, ws, bsb, w_out, layer, j, seq):
    tokens = x2.shape[0]
    tm = TOKEN_TILE
    per_seq = seq // tm
    return pl.pallas_call(
        _sgu_kernel,
        grid=(tokens // tm,),
        in_specs=[
            pl.BlockSpec((tm, D_MODEL), lambda i: (i, 0)),
            pl.BlockSpec((None, None, 6, D_MODEL), lambda i: (layer, i // per_seq, 0, 0)),
            _layer((4, D_MODEL), layer),
            _layer((D_MODEL, 2 * G_HALF), j),
            _layer((1, 2 * G_HALF), j),
            _layer((1, G_HALF), j),
            _layer((1, G_HALF), j),
            _layer((G_GROUPS, G_CHUNK, G_CHUNK), j),
            _layer((G_GROUPS, G_CHUNK, LANES), j),
            _layer((G_HALF, D_MODEL), j),
        ],
        out_specs=pl.BlockSpec((tm, D_MODEL), lambda i: (i, 0)),
        out_shape=jax.ShapeDtypeStruct(x2.shape, F32),
        scratch_shapes=[pltpu.VMEM((tm, G_HALF), BF16)],
        compiler_params=_params(("arbitrary",)),
        name="sgu_mixer",
    )(x2, mods, norm_g, w_in, b_in, ln_g, ln_b, ws, bsb, w_out)


M_ARG_BLOCK = LANES
M_ARG_COLS = M_HEADS * M_ARG_BLOCK
_T_NEG_M = 0
_T_NEG_B = 3 * M_HEADS
_T_ONES = 6 * M_HEADS
_T_END = 9 * M_HEADS
_R_DATA_ROWS = 32


def _selector_const():
    r = np.zeros((LANES, M_ARG_COLS), np.float32)
    for hd in range(M_HEADS):
        c0 = hd * M_ARG_BLOCK
        for part in range(3):
            r[_T_NEG_M + part * M_HEADS + hd, c0:c0 + M_ARG_BLOCK] = 1.0
    return r


def _split3(v):
    hi = v.astype(BF16).astype(F32)
    r1 = v - hi
    mid = r1.astype(BF16).astype(F32)
    lo = (r1 - mid).astype(BF16).astype(F32)
    return hi, mid, lo


def _segment_scan_lanes(v, op, identity, seg):
    lane = lax.rem(lax.broadcasted_iota(jnp.int32, v.shape, v.ndim - 1), seg)
    s = 1
    while s < seg:
        v = op(v, jnp.where(lane >= s, pltpu.roll(v, s, axis=v.ndim - 1), identity))
        s *= 2
    return v


def _mlstm_kernel(x_ref, mod_ref, ng_ref, win_ref, wg_ref, bg_ref, hg_ref, sel_ref,
                  wout_ref, o_ref, p_scr, y_scr, s_scr, msub_scr,
                  rhs_scr, e_scr, a_scr, kt_scr, hh_scr):
    L = M_CHUNK
    ts = x_ref.shape[0]

    @pl.when(pl.program_id(1) == 0)
    def _():
        s_scr[...] = jnp.zeros(s_scr.shape, F32)
        msub_scr[...] = jnp.zeros(msub_scr.shape, F32)

    row = lax.broadcasted_iota(jnp.int32, (L, L), 0)
    col = lax.broadcasted_iota(jnp.int32, (L, L), 1)
    causal = col <= row
    sub8 = lax.broadcasted_iota(jnp.int32, (M_HEADS, LANES), 0)
    ones_blk = jnp.ones((L, M_DV), BF16)
    zeros_k = jnp.zeros((M_DK, L), F32)
    zeros_s = jnp.zeros((M_DK, 2 * M_DV), F32)
    zeros_r = jnp.zeros((M_HEADS, LANES), F32)

    def chunk(c):
        rows = slice(c * L, (c + 1) * L)

        bc_row = bc_all[:, rows]
        r_row = r_all[:, rows]
        cmax = cmax_all[:, rows]
        m_st = msub_scr[...]
        mx = jnp.maximum(cmax, m_st)
        rmax = cmax[:, L - 1:L]
        wgt_row = jnp.exp(r_row - rmax)
        b_last = bc_row[:, L - 1:L]
        m_loc = b_last + rmax
        m_new = jnp.maximum(b_last + m_st, m_loc)
        s_old = jnp.exp(b_last + m_st - m_new)
        s_new = jnp.exp(m_loc - m_new)
        msub_scr[...] = m_new

        pieces = (list(_split3(-LOG2_E * mx)) + [jnp.zeros((_T_ONES - _T_NEG_B, L), F32)]
                  + [jnp.ones((_T_END - _T_ONES, L), F32), jnp.zeros((LANES - _T_END, L), F32)])
        t = jnp.concatenate(pieces, axis=0).T

        r_parts = _split3(LOG2_E * r_row)
        data = []
        for part in range(3):
            data.append(jnp.concatenate(
                [jnp.where(sub8 == hd, r_parts[part], 0.0) for hd in range(M_HEADS)], axis=1))
        data.append(jnp.zeros((M_HEADS, M_ARG_COLS), F32))
        rhs_scr[c, _T_ONES:_T_ONES + _R_DATA_ROWS, :] = (
            jnp.concatenate(data, axis=0).astype(BF16))
        e_scr[c] = jnp.exp2(_dot(t.astype(BF16), rhs_scr[c]))

        per_row = jnp.concatenate(
            [jnp.exp2(LOG2_E * (m_st - mx)), jnp.exp2(-LOG2_E * (bc_row + mx)),
             jnp.zeros((LANES - 2 * M_HEADS, L), F32)], axis=0).T

        for pair in range(M_HEADS // 2):
            pc = pair * LANES
            q2f = p_scr[rows, pc:pc + LANES]
            kT = (p_scr[rows, M_QK + pc:M_QK + pc + LANES] * (M_DK ** -0.5)).T
            kt_scr[c, pair] = kT
            bd = jnp.concatenate(
                [jnp.concatenate([kT[0:M_DK], zeros_k], axis=0),
                 jnp.concatenate([zeros_k, kT[M_DK:2 * M_DK]], axis=0)], axis=1)
            s_pair = _dot(q2f.astype(BF16), bd.astype(BF16))
            for half in range(2):
                hd = 2 * pair + half
                e0 = hd * M_ARG_BLOCK
                decay = jnp.where(causal, e_scr[c, :, e0:e0 + LANES], 0.0)
                scores = (s_pair[:, half * L:(half + 1) * L] * decay).astype(BF16)
                qi = (q2f * jnp.broadcast_to(per_row[:, hd:hd + 1], (L, LANES))).astype(BF16)
                a_scr[c, hd] = jnp.concatenate([scores, qi], axis=1)

        for hd in range(M_HEADS):
            pair, half = divmod(hd, 2)
            e0 = hd * M_ARG_BLOCK
            vc = 2 * M_QK + hd * M_DV
            vext = jnp.concatenate([p_scr[rows, vc:vc + M_DV].astype(BF16), ones_blk], axis=1)
            s_h = s_scr[pair, half * M_DK:(half + 1) * M_DK, :]
            sz = jnp.concatenate([s_h, zeros_s] if half == 0 else [zeros_s, s_h], axis=0)
            nd = _dot(a_scr[c, hd], jnp.concatenate([vext, sz.astype(BF16)], axis=0))
            floor = jnp.broadcast_to(per_row[:, M_HEADS + hd:M_HEADS + hd + 1], (L, LANES))
            hh_scr[c, :, hd * M_DV:(hd + 1) * M_DV] = nd[:, :M_DV] / jnp.maximum(
                jnp.abs(nd[:, M_DV:]), floor)

        for hd in range(M_HEADS):
            hc = hd * M_DV
            oc = 2 * M_QK + D_MODEL + hc
            hh = hh_scr[c, :, hc:hc + M_DV]
            hh = hh * lax.rsqrt(jnp.mean(hh * hh, axis=-1, keepdims=True) + EPS)
            hh = hh * hg_ref[:, hc:hc + M_DV]
            y_scr[rows, hc:hc + M_DV] = (jax.nn.sigmoid(p_scr[rows, oc:oc + M_DV]) * hh).astype(BF16)

        for hd in range(M_HEADS):
            pair, half = divmod(hd, 2)
            vc = 2 * M_QK + hd * M_DV
            vext = jnp.concatenate([p_scr[rows, vc:vc + M_DV].astype(BF16), ones_blk], axis=1)
            kT_h = kt_scr[c, pair, half * M_DK:(half + 1) * M_DK, :]
            upd = _dot((kT_h * wgt_row[hd:hd + 1, :]).astype(BF16), vext)
            s_h = s_scr[pair, half * M_DK:(half + 1) * M_DK, :]
            s_scr[pair, half * M_DK:(half + 1) * M_DK, :] = (
                jnp.concatenate([s_old[hd:hd + 1, :], s_old[hd:hd + 1, :]], axis=1) * s_h
                + jnp.concatenate([s_new[hd:hd + 1, :], s_new[hd:hd + 1, :]], axis=1) * upd)

    x = x_ref[...]
    mod = mod_ref[...]
    ng = ng_ref[...]
    h = (_rms(x, ng[0:1] * (1.0 + mod[1:2])) + mod[0:1]).astype(BF16)
    g_t = (_wdot(h, wg_ref[...]) + bg_ref[...]).T
    ii_all = GATE_CAP * jnp.tanh(g_t[0:M_HEADS] / GATE_CAP)
    logf = jax.nn.log_sigmoid(GATE_CAP * jnp.tanh(g_t[M_HEADS:2 * M_HEADS] / GATE_CAP))
    bc_all = _segment_scan_lanes(logf, jnp.add, 0.0, L)
    r_all = ii_all - bc_all
    cmax_all = _segment_scan_lanes(r_all, jnp.maximum, -jnp.inf, L)
    last = M_QKVO - MXU_COLS
    p_scr[:, :last] = _wdot(h, win_ref[:, :last])
    scanned = _zero_words(cmax_all)
    scanned = scanned[:, :MXU_COLS] | scanned[:, MXU_COLS:]
    p_scr[:, last:] = _wdot(h, _after(win_ref[:, last:], scanned))
    for c in range(ts // L):
        rhs_scr[c] = sel_ref[...]
    for c in range(ts // L):
        chunk(c)
    y = _wdot(y_scr[...], wout_ref[...])
    o_ref[...] = x + _rms(y, mod[2:3] * ng[1:2])


def _pad_lanes(a):
    return jnp.pad(a, [(0, 0)] * (a.ndim - 1) + [(0, LANES - a.shape[-1])])


def _mlstm(x2, mods, norm_g, w_in, w_g, b_g, hnorm_g, w_out, layer, j, bsz, seq):
    ts = TOKEN_TILE
    per_seq = seq // ts
    sel = jnp.asarray(_selector_const(), BF16)
    chunks = ts // M_CHUNK
    return pl.pallas_call(
        _mlstm_kernel,
        grid=(bsz, per_seq),
        in_specs=[
            pl.BlockSpec((ts, D_MODEL), lambda b, t: (b * per_seq + t, 0)),
            pl.BlockSpec((None, None, 6, D_MODEL), lambda b, t: (layer, b, 0, 0)),
            _layer((4, D_MODEL), layer),
            _layer((D_MODEL, M_QKVO), j),
            _layer((D_MODEL, LANES), j),
            _layer((1, LANES), j),
            _layer((1, D_MODEL), j),
            _resident((LANES, M_ARG_COLS)),
            _layer((D_MODEL, D_MODEL), j),
        ],
        out_specs=pl.BlockSpec((ts, D_MODEL), lambda b, t: (b * per_seq + t, 0)),
        out_shape=jax.ShapeDtypeStruct(x2.shape, F32),
        scratch_shapes=[
            pltpu.VMEM((ts, M_QKVO), F32),
            pltpu.VMEM((ts, D_MODEL), BF16),
            pltpu.VMEM((M_HEADS // 2, 2 * M_DK, 2 * M_DV), F32),
            pltpu.VMEM((M_HEADS, LANES), F32),
            pltpu.VMEM((chunks, LANES, M_ARG_COLS), BF16),
            pltpu.VMEM((chunks, M_CHUNK, M_ARG_COLS), F32),
            pltpu.VMEM((chunks, M_HEADS, M_CHUNK, 2 * LANES), BF16),
            pltpu.VMEM((chunks, M_HEADS // 2, LANES, M_CHUNK), F32),
            pltpu.VMEM((chunks, M_CHUNK, D_MODEL), F32),
        ],
        compiler_params=_params(("arbitrary", "arbitrary")),
        name="mlstm_mixer",
    )(x2, mods, norm_g, w_in, w_g, b_g, hnorm_g, sel, w_out)


def kernel(x, c, norm_g, ada_w, ada_b, ffn_w1, ffn_w2, a_w_in, a_b_if, a_hnorm_g, a_w_out,
           b_w_in, b_b_in, b_ln_g, b_ln_b, b_ws, b_bs, b_w_out):
    bsz, seq, d = x.shape
    assert d == D_MODEL and seq % TOKEN_TILE == 0 and norm_g.shape[0] == DEPTH
    mods = _ada(c, ada_w, ada_b).reshape(DEPTH, bsz, 6, D_MODEL)
    a_w_g = _pad_lanes(a_w_in[..., M_QKVO:])
    a_b_g = _pad_lanes(a_b_if)[:, None, :]
    a_hnorm_g = a_hnorm_g[:, None, :]
    b_b_in, b_ln_g, b_ln_b = b_b_in[:, None, :], b_ln_g[:, None, :], b_ln_b[:, None, :]
    b_bsb = jnp.broadcast_to(0.5 * b_bs[..., None], b_bs.shape + (LANES,))
    x2 = x.reshape(bsz * seq, D_MODEL)
    for i in range(DEPTH):
        j = i // 2
        if i % 2 == 0:
            x2 = _mlstm(x2, mods, norm_g, a_w_in, a_w_g, a_b_g, a_hnorm_g, a_w_out,
                        i, j, bsz, seq)
        else:
            x2 = _sgu(x2, mods, norm_g, b_w_in, b_b_in, b_ln_g, b_ln_b, b_ws, b_bsb, b_w_out,
                      i, j, seq)
        x2 = _ffn(x2, mods, norm_g, ffn_w1, ffn_w2, i, seq)
    return x2.reshape(bsz, seq, D_MODEL)
```

```python
import functools

import jax
import jax.numpy as jnp
from jax import lax
from jax.experimental import pallas as pl
from jax.experimental.pallas import tpu as pltpu

F32 = jnp.float32
BF16 = jnp.bfloat16

D_MODEL = 1024
DEPTH = 4
EPS = 1e-6

M_HEADS = 8
M_DV = 128
M_DK = 64
M_CHUNK = 128
GATE_CAP = 15.0
LOG2_E = 1.4426950408889634
M_QK = M_HEADS * M_DK
M_QKVO = 2 * M_QK + 2 * D_MODEL

G_CHUNK = 128
G_HALF = 2048
G_GROUP = 128
G_GROUPS = 16

FFN_HIDDEN = 4096

LANES = 128
MXU_COLS = 256
VMEM_LIMIT_BYTES = 60 * 1024 * 1024

TOKEN_TILE = 512
FFN_HIDDEN_TILE = 1024


def _dot(a, b):
    return jnp.dot(a, b, preferred_element_type=F32)


def _wdot(a, w):
    return _dot(a, w.astype(BF16))


def _rms(x, g):
    return x * lax.rsqrt(jnp.mean(x * x, axis=-1, keepdims=True) + EPS) * g


def _zero_words(v):
    u = pltpu.bitcast(v, jnp.uint32)
    acc = u[0:8]
    for r in range(8, u.shape[0], 8):
        acc = acc | u[r:r + 8]
    return (acc >> 16) >> 16


def _after(w, zero_words):
    rows = 8 * 4 // w.dtype.itemsize
    k = zero_words.shape[1]
    top = pltpu.bitcast(pltpu.bitcast(w[0:rows, 0:k], jnp.uint32) | zero_words, w.dtype)
    if k < w.shape[1]:
        top = jnp.concatenate([top, w[0:rows, k:]], axis=1)
    return jnp.concatenate([top, w[rows:]], axis=0)


def _layer(shape, layer):
    index = (layer,) + (0,) * len(shape)
    return pl.BlockSpec((None,) + tuple(shape), lambda *_: index, pipeline_mode=pl.Buffered(1))


def _params(semantics, flags=None):
    return pltpu.CompilerParams(dimension_semantics=semantics,
                                vmem_limit_bytes=VMEM_LIMIT_BYTES, flags=flags)


def _ada_kernel(c_ref, w_ref, b_ref, o_ref):
    c = c_ref[...]
    ca = (c * jax.nn.sigmoid(c)).astype(BF16)
    o_ref[...] = _wdot(ca, w_ref[...]) + b_ref[...]


def _ada(c, ada_w, ada_b):
    bsz = c.shape[0]
    blk = 2 * D_MODEL
    return pl.pallas_call(
        _ada_kernel,
        grid=(DEPTH, 6 * D_MODEL // blk),
        in_specs=[
            pl.BlockSpec((bsz, D_MODEL), lambda i, j: (0, 0)),
            pl.BlockSpec((None, D_MODEL, blk), lambda i, j: (i, 0, j)),
            pl.BlockSpec((None, 1, blk), lambda i, j: (i, 0, j)),
        ],
        out_specs=pl.BlockSpec((None, bsz, blk), lambda i, j: (i, 0, j)),
        out_shape=jax.ShapeDtypeStruct((DEPTH, bsz, 6 * D_MODEL), F32),
        compiler_params=_params(("arbitrary", "arbitrary")),
        name="ada_mod",
    )(c, ada_w, ada_b.reshape(DEPTH, 1, 6 * D_MODEL))


_MIXER_ROWS = (0, 0, 1, 1, 2)
_FFN_ROWS = (2, 3, 4, 3, 5)


def _pipelined(n, piece, tiles, refs, rows, core, start_fetch):
    xn_ref, xp_ref, modn_ref, modp_ref, ng_ref, o_ref, h_scr, y_scr = refs
    g_in, shift, scale, g_out, gate = rows
    i = pl.program_id(0)
    tm = xn_ref.shape[0]
    ng = ng_ref[...]

    def normalise(x_ref, mod_ref, slot, r0):
        mod = mod_ref[...]
        gain = ng[g_in:g_in + 1] * (1.0 + mod[scale:scale + 1])
        hn = (_rms(x_ref[r0:r0 + piece, :], gain) + mod[shift:shift + 1]).astype(BF16)
        h_scr[slot, r0:r0 + piece, :] = hn
        return hn

    def finish(slot, r0):
        gain = modp_ref[gate:gate + 1, :] * ng[g_out:g_out + 1]
        o = xp_ref[r0:r0 + piece, :] + _rms(y_scr[slot, r0:r0 + piece, :], gain)
        o_ref[r0:r0 + piece, :] = o
        return o

    @pl.when(i == 0)
    def _():
        start_fetch()
        for r0 in range(0, tm, piece):
            normalise(xp_ref, modp_ref, 0, r0)
        y_scr[1] = jnp.zeros(y_scr.shape[1:], F32)

    def step(slot, first):
        work = []
        for r0 in range(0, tm, piece):
            work.append(functools.partial(finish, 1 - slot, r0))
            work.append(functools.partial(normalise, xn_ref, modn_ref, 1 - slot, r0))

        left = [tiles]

        def anchored(w):
            take = -(-len(work) // left[0])
            left[0] -= 1
            for _ in range(take):
                w = _after(w, _zero_words(work.pop(0)())[:, :w.shape[1]])
            return w

        core(slot, anchored, first)
        assert not work and left[0] == 0, "core must pass exactly `tiles` weight tiles"

    even = lax.rem(i, 2) == 0
    pl.when(i == 0)(lambda: step(0, True))
    pl.when((i > 0) & (i < n) & even)(lambda: step(0, False))
    pl.when((i < n) & jnp.logical_not(even))(lambda: step(1, False))

    @pl.when(i == n)
    def _():
        for r0 in range(0, tm, piece):
            finish((n - 1) % 2, r0)


def _pipelined_specs(n, tm, per_seq, layer):
    nxt = lambda i: jnp.minimum(i + 1, n - 1)
    prv = lambda i: jnp.maximum(i - 1, 0)
    in_specs = [
        pl.BlockSpec((tm, D_MODEL), lambda i: (nxt(i), 0)),
        pl.BlockSpec((tm, D_MODEL), lambda i: (prv(i), 0)),
        pl.BlockSpec((None, None, 6, D_MODEL), lambda i: (layer, nxt(i) // per_seq, 0, 0)),
        pl.BlockSpec((None, None, 6, D_MODEL), lambda i: (layer, prv(i) // per_seq, 0, 0)),
        _layer((4, D_MODEL), layer),
    ]
    return in_specs, pl.BlockSpec((tm, D_MODEL), lambda i: (prv(i), 0))


def _pipelined_scratch(tm):
    return [pltpu.VMEM((2, tm, D_MODEL), BF16),
            pltpu.VMEM((2, tm, D_MODEL), F32)]


def _ffn_kernel(xn_ref, xp_ref, modn_ref, modp_ref, ng_ref, w1_hbm, w2_hbm, o_ref,
                h_scr, y_scr, w1_ref, w2_ref, sem, *, n, layer):
    chunks = list(range(0, FFN_HIDDEN, FFN_HIDDEN_TILE))

    def fetch(which, k):
        k0 = chunks[k]
        if which == 0:
            src = w1_hbm.at[layer, :, k0:k0 + FFN_HIDDEN_TILE]
            dst = w1_ref.at[:, k0:k0 + FFN_HIDDEN_TILE]
        else:
            src = w2_hbm.at[layer, k0:k0 + FFN_HIDDEN_TILE, :]
            dst = w2_ref.at[k0:k0 + FFN_HIDDEN_TILE, :]
        return pltpu.make_async_copy(src, dst, sem.at[which, k])

    def start_fetch():
        for k in range(len(chunks)):
            fetch(0, k).start()
            fetch(1, k).start()

    def core(slot, anchored, first):
        h = h_scr[slot]
        y = [None] * (D_MODEL // MXU_COLS)
        for k, k0 in enumerate(chunks):
            if first:
                fetch(0, k).wait()
            a = []
            for c0 in range(k0, k0 + FFN_HIDDEN_TILE, MXU_COLS):
                aj = _wdot(h, anchored(w1_ref[:, c0:c0 + MXU_COLS]))
                a.append(jnp.square(jnp.maximum(aj, 0.0)).astype(BF16))
            a = jnp.concatenate(a, axis=1)
            if first:
                fetch(1, k).wait()
            for j, c0 in enumerate(range(0, D_MODEL, MXU_COLS)):
                yk = _wdot(a, anchored(w2_ref[k0:k0 + FFN_HIDDEN_TILE, c0:c0 + MXU_COLS]))
                y[j] = yk if y[j] is None else y[j] + yk
        y_scr[slot] = jnp.concatenate(y, axis=1)

    tiles = (FFN_HIDDEN // MXU_COLS) + (FFN_HIDDEN // FFN_HIDDEN_TILE) * (D_MODEL // MXU_COLS)
    _pipelined(n, 32, tiles, (xn_ref, xp_ref, modn_ref, modp_ref, ng_ref, o_ref, h_scr, y_scr),
               _FFN_ROWS, core, start_fetch)


def _ffn(x2, mods, norm_g, w1, w2, layer, seq):
    tokens = x2.shape[0]
    tm = TOKEN_TILE
    per_seq = seq // tm
    n = tokens // tm
    in_specs, out_spec = _pipelined_specs(n, tm, per_seq, layer)
    return pl.pallas_call(
        functools.partial(_ffn_kernel, n=n, layer=layer),
        grid=(n + 1,),
        in_specs=in_specs + [
            pl.BlockSpec(memory_space=pl.ANY),
            pl.BlockSpec(memory_space=pl.ANY),
        ],
        out_specs=out_spec,
        out_shape=jax.ShapeDtypeStruct(x2.shape, F32),
        scratch_shapes=_pipelined_scratch(tm) + [
            pltpu.VMEM((D_MODEL, FFN_HIDDEN), F32),
            pltpu.VMEM((FFN_HIDDEN, D_MODEL), F32),
            pltpu.SemaphoreType.DMA((2, FFN_HIDDEN // FFN_HIDDEN_TILE)),
        ],
        compiler_params=_params(("arbitrary",)),
        name="ffn_relu2",
    )(x2, x2, mods, mods, norm_g, w1, w2)


def _gelu_x2(z):
    return z * (1.0 + lax.erf(z * (2.0 ** -0.5)))


def _sgu_kernel(x_ref, mod_ref, ng_ref, win_ref, bin_ref, lng_ref, lnb_ref,
                ws_ref, bsb_ref, wout_ref, o_ref, y_scr):
    x = x_ref[...]
    mod = mod_ref[...]
    ng = ng_ref[...]
    tm = x.shape[0]
    h = (_rms(x, ng[0:1] * (1.0 + mod[1:2])) + mod[0:1]).astype(BF16)
    zv = _gelu_x2(_wdot(h, win_ref[:, G_HALF:]) + bin_ref[:, G_HALF:])
    mu = jnp.mean(zv, axis=-1, keepdims=True)
    zc = zv - mu
    var = jnp.mean(zc * zc, axis=-1, keepdims=True)
    vn = (zc * lax.rsqrt(var + 4.0 * EPS) * lng_ref[...] + lnb_ref[...]).astype(BF16)
    zu = _gelu_x2(_wdot(h, win_ref[:, :G_HALF]) + bin_ref[:, :G_HALF])

    row = lax.broadcasted_iota(jnp.int32, (G_CHUNK, G_CHUNK), 0)
    col = lax.broadcasted_iota(jnp.int32, (G_CHUNK, G_CHUNK), 1)
    causal = col <= row
    for g in range(G_GROUPS):
        c0 = g * G_GROUP
        wm = jnp.where(causal, 0.5 * ws_ref[g], 0.0).astype(BF16)
        bsg = bsb_ref[g]
        for r0 in range(0, tm, 2 * G_CHUNK):
            r1 = r0 + G_CHUNK
            r2 = r1 + G_CHUNK
            rhs = jnp.concatenate([vn[r0:r1, c0:c0 + G_GROUP],
                                   vn[r1:r2, c0:c0 + G_GROUP]], axis=1)
            sv = _dot(wm, rhs)
            y_scr[r0:r1, c0:c0 + G_GROUP] = (
                zu[r0:r1, c0:c0 + G_GROUP] * (sv[:, :G_GROUP] + bsg)).astype(BF16)
            y_scr[r1:r2, c0:c0 + G_GROUP] = (
                zu[r1:r2, c0:c0 + G_GROUP] * (sv[:, G_GROUP:] + bsg)).astype(BF16)
    y = _wdot(y_scr[...], wout_ref[...])
    o_ref[...] = x + _rms(y, mod[2:3] * ng[1:2])


def _sgu(x2, mods, norm_g, w_in, b_in, ln_g, ln_b, ws, bsb, w_out, layer, j, seq):
    tokens = x2.shape[0]
    tm = TOKEN_TILE
    per_seq = seq // tm
    return pl.pallas_call(
        _sgu_kernel,
        grid=(tokens // tm,),
        in_specs=[
            pl.BlockSpec((tm, D_MODEL), lambda i: (i, 0)),
            pl.BlockSpec((None, None, 6, D_MODEL), lambda i: (layer, i // per_seq, 0, 0)),
            _layer((4, D_MODEL), layer),
            _layer((D_MODEL, 2 * G_HALF), j),
            _layer((1, 2 * G_HALF), j),
            _layer((1, G_HALF), j),
            _layer((1, G_HALF), j),
            _layer((G_GROUPS, G_CHUNK, G_CHUNK), j),
            _layer((G_GROUPS, G_CHUNK, LANES), j),
            _layer((G_HALF, D_MODEL), j),
        ],
        out_specs=pl.BlockSpec((tm, D_MODEL), lambda i: (i, 0)),
        out_shape=jax.ShapeDtypeStruct(x2.shape, F32),
        scratch_shapes=[pltpu.VMEM((tm, G_HALF), BF16)],
        compiler_params=_params(("arbitrary",)),
        name="sgu_mixer",
    )(x2, mods, norm_g, w_in, b_in, ln_g, ln_b, ws, bsb, w_out)


def _segment_scan_lanes(v, op, identity, seg):
    lane = lax.rem(lax.broadcasted_iota(jnp.int32, v.shape, v.ndim - 1), seg)
    s = 1
    while s < seg:
        v = op(v, jnp.where(lane >= s, pltpu.roll(v, s, axis=v.ndim - 1), identity))
        s *= 2
    return v


def _mlstm_kernel(x_ref, mod_ref, ng_ref, win_ref, wg_ref, bg_ref, hg_ref,
                  wout_ref, o_ref, p_scr, y_scr, s_scr, msub_scr, a_scr, kt_scr, hh_scr):
    L = M_CHUNK
    ts = x_ref.shape[0]

    @pl.when(pl.program_id(1) == 0)
    def _():
        s_scr[...] = jnp.zeros(s_scr.shape, F32)
        msub_scr[...] = jnp.zeros(msub_scr.shape, F32)

    row = lax.broadcasted_iota(jnp.int32, (L, L), 0)
    col = lax.broadcasted_iota(jnp.int32, (L, L), 1)
    causal = col <= row
    ones_blk = jnp.ones((L, M_DV), BF16)
    zeros_k = jnp.zeros((M_DK, L), F32)
    zeros_s = jnp.zeros((M_DK, 2 * M_DV), F32)

    def chunk(c):
        rows = slice(c * L, (c + 1) * L)

        bc_row = bc_all[:, rows]
        r_row = r_all[:, rows]
        cmax = cmax_all[:, rows]
        m_st = msub_scr[...]
        mx = jnp.maximum(cmax, m_st)
        rmax = cmax[:, L - 1:L]
        wgt_row = jnp.exp(r_row - rmax)
        b_last = bc_row[:, L - 1:L]
        m_loc = b_last + rmax
        m_new = jnp.maximum(b_last + m_st, m_loc)
        s_old = jnp.exp(b_last + m_st - m_new)
        s_new = jnp.exp(m_loc - m_new)
        msub_scr[...] = m_new

        r2_row = LOG2_E * r_row
        per_row = jnp.concatenate(
            [jnp.exp2(LOG2_E * (m_st - mx)), jnp.exp2(-LOG2_E * (bc_row + mx)), LOG2_E * mx,
             jnp.zeros((LANES - 3 * M_HEADS, L), F32)], axis=0).T

        for pair in range(M_HEADS // 2):
            pc = pair * LANES
            q2f = p_scr[rows, pc:pc + LANES]
            kT = (p_scr[rows, M_QK + pc:M_QK + pc + LANES] * (M_DK ** -0.5)).T
            kt_scr[c, pair] = kT
            bd = jnp.concatenate(
                [jnp.concatenate([kT[0:M_DK], zeros_k], axis=0),
                 jnp.concatenate([zeros_k, kT[M_DK:2 * M_DK]], axis=0)], axis=1)
            s_pair = _dot(q2f.astype(BF16), bd.astype(BF16))
            for half in range(2):
                hd = 2 * pair + half
                m_rep = jnp.broadcast_to(per_row[:, 2 * M_HEADS + hd:2 * M_HEADS + hd + 1],
                                         (L, LANES))
                decay = jnp.where(causal, jnp.exp2(r2_row[hd:hd + 1, :] - m_rep), 0.0)
                scores = (s_pair[:, half * L:(half + 1) * L] * decay).astype(BF16)
                qi = (q2f * jnp.broadcast_to(per_row[:, hd:hd + 1], (L, LANES))).astype(BF16)
                a_scr[c, hd] = jnp.concatenate([scores, qi], axis=1)

        for hd in range(M_HEADS):
            pair, half = divmod(hd, 2)
            vc = 2 * M_QK + hd * M_DV
            vext = jnp.concatenate([p_scr[rows, vc:vc + M_DV].astype(BF16), ones_blk], axis=1)
            s_h = s_scr[pair, half * M_DK:(half + 1) * M_DK, :]
            sz = jnp.concatenate([s_h, zeros_s] if half == 0 else [zeros_s, s_h], axis=0)
            nd = _dot(a_scr[c, hd], jnp.concatenate([vext, sz.astype(BF16)], axis=0))
            floor = jnp.broadcast_to(per_row[:, M_HEADS + hd:M_HEADS + hd + 1], (L, LANES))
            hh_scr[c, :, hd * M_DV:(hd + 1) * M_DV] = nd[:, :M_DV] / jnp.maximum(
                jnp.abs(nd[:, M_DV:]), floor)

        for hd in range(M_HEADS):
            hc = hd * M_DV
            oc = 2 * M_QK + D_MODEL + hc
            hh = hh_scr[c, :, hc:hc + M_DV]
            hh = hh * lax.rsqrt(jnp.mean(hh * hh, axis=-1, keepdims=True) + EPS)
            hh = hh * hg_ref[:, hc:hc + M_DV]
            y_scr[rows, hc:hc + M_DV] = (jax.nn.sigmoid(p_scr[rows, oc:oc + M_DV]) * hh).astype(BF16)

        for hd in range(M_HEADS):
            pair, half = divmod(hd, 2)
            vc = 2 * M_QK + hd * M_DV
            vext = jnp.concatenate([p_scr[rows, vc:vc + M_DV].astype(BF16), ones_blk], axis=1)
            kT_h = kt_scr[c, pair, half * M_DK:(half + 1) * M_DK, :]
            upd = _dot((kT_h * wgt_row[hd:hd + 1, :]).astype(BF16), vext)
            s_h = s_scr[pair, half * M_DK:(half + 1) * M_DK, :]
            s_scr[pair, half * M_DK:(half + 1) * M_DK, :] = (
                jnp.concatenate([s_old[hd:hd + 1, :], s_old[hd:hd + 1, :]], axis=1) * s_h
                + jnp.concatenate([s_new[hd:hd + 1, :], s_new[hd:hd + 1, :]], axis=1) * upd)

    x = x_ref[...]
    mod = mod_ref[...]
    ng = ng_ref[...]
    h = (_rms(x, ng[0:1] * (1.0 + mod[1:2])) + mod[0:1]).astype(BF16)
    g_t = (_wdot(h, wg_ref[...]) + bg_ref[...]).T
    ii_all = GATE_CAP * jnp.tanh(g_t[0:M_HEADS] / GATE_CAP)
    logf = jax.nn.log_sigmoid(GATE_CAP * jnp.tanh(g_t[M_HEADS:2 * M_HEADS] / GATE_CAP))
    bc_all = _segment_scan_lanes(logf, jnp.add, 0.0, L)
    r_all = ii_all - bc_all
    cmax_all = _segment_scan_lanes(r_all, jnp.maximum, -jnp.inf, L)
    last = M_QKVO - MXU_COLS
    p_scr[:, :last] = _wdot(h, win_ref[:, :last])
    scanned = _zero_words(cmax_all)
    scanned = scanned[:, :MXU_COLS] | scanned[:, MXU_COLS:]
    p_scr[:, last:] = _wdot(h, _after(win_ref[:, last:], scanned))
    for c in range(ts // L):
        chunk(c)
    y = _wdot(y_scr[...], wout_ref[...])
    o_ref[...] = x + _rms(y, mod[2:3] * ng[1:2])


def _pad_lanes(a):
    return jnp.pad(a, [(0, 0)] * (a.ndim - 1) + [(0, LANES - a.shape[-1])])


def _mlstm(x2, mods, norm_g, w_in, w_g, b_g, hnorm_g, w_out, layer, j, bsz, seq):
    ts = TOKEN_TILE
    per_seq = seq // ts
    chunks = ts // M_CHUNK
    return pl.pallas_call(
        _mlstm_kernel,
        grid=(bsz, per_seq),
        in_specs=[
            pl.BlockSpec((ts, D_MODEL), lambda b, t: (b * per_seq + t, 0)),
            pl.BlockSpec((None, None, 6, D_MODEL), lambda b, t: (layer, b, 0, 0)),
            _layer((4, D_MODEL), layer),
            _layer((D_MODEL, M_QKVO), j),
            _layer((D_MODEL, LANES), j),
            _layer((1, LANES), j),
            _layer((1, D_MODEL), j),
            _layer((D_MODEL, D_MODEL), j),
        ],
        out_specs=pl.BlockSpec((ts, D_MODEL), lambda b, t: (b * per_seq + t, 0)),
        out_shape=jax.ShapeDtypeStruct(x2.shape, F32),
        scratch_shapes=[
            pltpu.VMEM((ts, M_QKVO), F32),
            pltpu.VMEM((ts, D_MODEL), BF16),
            pltpu.VMEM((M_HEADS // 2, 2 * M_DK, 2 * M_DV), F32),
            pltpu.VMEM((M_HEADS, LANES), F32),
            pltpu.VMEM((chunks, M_HEADS, M_CHUNK, 2 * LANES), BF16),
            pltpu.VMEM((chunks, M_HEADS // 2, LANES, M_CHUNK), F32),
            pltpu.VMEM((chunks, M_CHUNK, D_MODEL), F32),
        ],
        compiler_params=_params(("arbitrary", "arbitrary")),
        name="mlstm_mixer",
    )(x2, mods, norm_g, w_in, w_g, b_g, hnorm_g, w_out)


def kernel(x, c, norm_g, ada_w, ada_b, ffn_w1, ffn_w2, a_w_in, a_b_if, a_hnorm_g, a_w_out,
           b_w_in, b_b_in, b_ln_g, b_ln_b, b_ws, b_bs, b_w_out):
    bsz, seq, d = x.shape
    assert d == D_MODEL and seq % TOKEN_TILE == 0 and norm_g.shape[0] == DEPTH
    mods = _ada(c, ada_w, ada_b).reshape(DEPTH, bsz, 6, D_MODEL)
    a_w_g = _pad_lanes(a_w_in[..., M_QKVO:])
    a_b_g = _pad_lanes(a_b_if)[:, None, :]
    a_hnorm_g = a_hnorm_g[:, None, :]
    b_b_in, b_ln_g, b_ln_b = b_b_in[:, None, :], b_ln_g[:, None, :], b_ln_b[:, None, :]
    b_bsb = jnp.broadcast_to(0.5 * b_bs[..., None], b_bs.shape + (LANES,))
    x2 = x.reshape(bsz * seq, D_MODEL)
    for i in range(DEPTH):
        j = i // 2
        if i % 2 == 0:
            x2 = _mlstm(x2, mods, norm_g, a_w_in, a_w_g, a_b_g, a_hnorm_g, a_w_out,
                        i, j, bsz, seq)
        else:
            x2 = _sgu(x2, mods, norm_g, b_w_in, b_b_in, b_ln_g, b_ln_b, b_ws, b_bsb, b_w_out,
                      i, j, seq)
        x2 = _ffn(x2, mods, norm_g, ffn_w1, ffn_w2, i, seq)
    return x2.reshape(bsz, seq, D_MODEL)
```

```python
import functools

import jax
import jax.numpy as jnp
from jax import lax
from jax.experimental import pallas as pl
from jax.experimental.pallas import tpu as pltpu

F32 = jnp.float32
BF16 = jnp.bfloat16

D_MODEL = 1024
DEPTH = 4
EPS = 1e-6

M_HEADS = 8
M_DV = 128
M_DK = 64
M_CHUNK = 128
GATE_CAP = 15.0
LOG2_E = 1.4426950408889634
M_QK = M_HEADS * M_DK
M_QKVO = 2 * M_QK + 2 * D_MODEL

G_CHUNK = 128
G_HALF = 2048
G_GROUP = 128
G_GROUPS = 16

FFN_HIDDEN = 4096

LANES = 128
MXU_COLS = 256
VMEM_LIMIT_BYTES = 60 * 1024 * 1024

TOKEN_TILE = 512
FFN_HIDDEN_TILE = 1024


def _dot(a, b):
    return jnp.dot(a, b, preferred_element_type=F32)


def _wdot(a, w):
    return _dot(a, w.astype(BF16))


def _rms(x, g):
    return x * lax.rsqrt(jnp.mean(x * x, axis=-1, keepdims=True) + EPS) * g


def _zero_words(v):
    u = pltpu.bitcast(v, jnp.uint32)
    acc = u[0:8]
    for r in range(8, u.shape[0], 8):
        acc = acc | u[r:r + 8]
    return (acc >> 16) >> 16


def _after(w, zero_words):
    rows = 8 * 4 // w.dtype.itemsize
    k = zero_words.shape[1]
    top = pltpu.bitcast(pltpu.bitcast(w[0:rows, 0:k], jnp.uint32) | zero_words, w.dtype)
    if k < w.shape[1]:
        top = jnp.concatenate([top, w[0:rows, k:]], axis=1)
    return jnp.concatenate([top, w[rows:]], axis=0)


def _layer(shape, layer):
    index = (layer,) + (0,) * len(shape)
    return pl.BlockSpec((None,) + tuple(shape), lambda *_: index, pipeline_mode=pl.Buffered(1))


def _params(semantics, flags=None):
    return pltpu.CompilerParams(dimension_semantics=semantics,
                                vmem_limit_bytes=VMEM_LIMIT_BYTES, flags=flags)


def _ada_kernel(c_ref, w_ref, b_ref, o_ref):
    c = c_ref[...]
    ca = (c * jax.nn.sigmoid(c)).astype(BF16)
    o_ref[...] = _wdot(ca, w_ref[...]) + b_ref[...]


def _ada(c, ada_w, ada_b):
    bsz = c.shape[0]
    blk = 2 * D_MODEL
    return pl.pallas_call(
        _ada_kernel,
        grid=(DEPTH, 6 * D_MODEL // blk),
        in_specs=[
            pl.BlockSpec((bsz, D_MODEL), lambda i, j: (0, 0)),
            pl.BlockSpec((None, D_MODEL, blk), lambda i, j: (i, 0, j)),
            pl.BlockSpec((None, 1, blk), lambda i, j: (i, 0, j)),
        ],
        out_specs=pl.BlockSpec((None, bsz, blk), lambda i, j: (i, 0, j)),
        out_shape=jax.ShapeDtypeStruct((DEPTH, bsz, 6 * D_MODEL), F32),
        compiler_params=_params(("arbitrary", "arbitrary")),
        name="ada_mod",
    )(c, ada_w, ada_b.reshape(DEPTH, 1, 6 * D_MODEL))


_MIXER_ROWS = (0, 0, 1, 1, 2)
_FFN_ROWS = (2, 3, 4, 3, 5)


def _pipelined(n, piece, tiles, refs, rows, core, start_fetch):
    xn_ref, xp_ref, modn_ref, modp_ref, ng_ref, o_ref, h_scr, y_scr = refs
    g_in, shift, scale, g_out, gate = rows
    i = pl.program_id(0)
    tm = xn_ref.shape[0]
    ng = ng_ref[...]

    def normalise(x_ref, mod_ref, slot, r0):
        mod = mod_ref[...]
        gain = ng[g_in:g_in + 1] * (1.0 + mod[scale:scale + 1])
        hn = (_rms(x_ref[r0:r0 + piece, :], gain) + mod[shift:shift + 1]).astype(BF16)
        h_scr[slot, r0:r0 + piece, :] = hn
        return hn

    def finish(slot, r0):
        gain = modp_ref[gate:gate + 1, :] * ng[g_out:g_out + 1]
        o = xp_ref[r0:r0 + piece, :] + _rms(y_scr[slot, r0:r0 + piece, :], gain)
        o_ref[r0:r0 + piece, :] = o
        return o

    @pl.when(i == 0)
    def _():
        start_fetch()
        for r0 in range(0, tm, piece):
            normalise(xp_ref, modp_ref, 0, r0)
        y_scr[1] = jnp.zeros(y_scr.shape[1:], F32)

    def step(slot, first):
        work = []
        for r0 in range(0, tm, piece):
            work.append(functools.partial(finish, 1 - slot, r0))
            work.append(functools.partial(normalise, xn_ref, modn_ref, 1 - slot, r0))

        left = [tiles]

        def anchored(w):
            take = -(-len(work) // left[0])
            left[0] -= 1
            for _ in range(take):
                w = _after(w, _zero_words(work.pop(0)())[:, :w.shape[1]])
            return w

        core(slot, anchored, first)
        assert not work and left[0] == 0, "core must pass exactly `tiles` weight tiles"

    even = lax.rem(i, 2) == 0
    pl.when(i == 0)(lambda: step(0, True))
    pl.when((i > 0) & (i < n) & even)(lambda: step(0, False))
    pl.when((i < n) & jnp.logical_not(even))(lambda: step(1, False))

    @pl.when(i == n)
    def _():
        for r0 in range(0, tm, piece):
            finish((n - 1) % 2, r0)


def _pipelined_specs(n, tm, per_seq, layer):
    nxt = lambda i: jnp.minimum(i + 1, n - 1)
    prv = lambda i: jnp.maximum(i - 1, 0)
    in_specs = [
        pl.BlockSpec((tm, D_MODEL), lambda i: (nxt(i), 0)),
        pl.BlockSpec((tm, D_MODEL), lambda i: (prv(i), 0)),
        pl.BlockSpec((None, None, 6, D_MODEL), lambda i: (layer, nxt(i) // per_seq, 0, 0)),
        pl.BlockSpec((None, None, 6, D_MODEL), lambda i: (layer, prv(i) // per_seq, 0, 0)),
        _layer((4, D_MODEL), layer),
    ]
    return in_specs, pl.BlockSpec((tm, D_MODEL), lambda i: (prv(i), 0))


def _pipelined_scratch(tm):
    return [pltpu.VMEM((2, tm, D_MODEL), BF16),
            pltpu.VMEM((2, tm, D_MODEL), F32)]


def _ffn_kernel(xn_ref, xp_ref, modn_ref, modp_ref, ng_ref, w1_hbm, w2_hbm, o_ref,
                h_scr, y_scr, w1_ref, w2_ref, sem, *, n, layer):
    chunks = list(range(0, FFN_HIDDEN, FFN_HIDDEN_TILE))

    def fetch(which, k):
        k0 = chunks[k]
        if which == 0:
            src = w1_hbm.at[layer, :, k0:k0 + FFN_HIDDEN_TILE]
            dst = w1_ref.at[:, k0:k0 + FFN_HIDDEN_TILE]
        else:
            src = w2_hbm.at[layer, k0:k0 + FFN_HIDDEN_TILE, :]
            dst = w2_ref.at[k0:k0 + FFN_HIDDEN_TILE, :]
        return pltpu.make_async_copy(src, dst, sem.at[which, k])

    def start_fetch():
        for k in range(len(chunks)):
            fetch(0, k).start()
            fetch(1, k).start()

    def core(slot, anchored, first):
        h = h_scr[slot]
        y = [None] * (D_MODEL // MXU_COLS)
        for k, k0 in enumerate(chunks):
            if first:
                fetch(0, k).wait()
            a = []
            for c0 in range(k0, k0 + FFN_HIDDEN_TILE, MXU_COLS):
                aj = _wdot(h, anchored(w1_ref[:, c0:c0 + MXU_COLS]))
                a.append(jnp.square(jnp.maximum(aj, 0.0)).astype(BF16))
            a = jnp.concatenate(a, axis=1)
            if first:
                fetch(1, k).wait()
            for j, c0 in enumerate(range(0, D_MODEL, MXU_COLS)):
                yk = _wdot(a, anchored(w2_ref[k0:k0 + FFN_HIDDEN_TILE, c0:c0 + MXU_COLS]))
                y[j] = yk if y[j] is None else y[j] + yk
        y_scr[slot] = jnp.concatenate(y, axis=1)

    tiles = (FFN_HIDDEN // MXU_COLS) + (FFN_HIDDEN // FFN_HIDDEN_TILE) * (D_MODEL // MXU_COLS)
    _pipelined(n, 32, tiles, (xn_ref, xp_ref, modn_ref, modp_ref, ng_ref, o_ref, h_scr, y_scr),
               _FFN_ROWS, core, start_fetch)


def _ffn(x2, mods, norm_g, w1, w2, layer, seq):
    tokens = x2.shape[0]
    tm = TOKEN_TILE
    per_seq = seq // tm
    n = tokens // tm
    in_specs, out_spec = _pipelined_specs(n, tm, per_seq, layer)
    return pl.pallas_call(
        functools.partial(_ffn_kernel, n=n, layer=layer),
        grid=(n + 1,),
        in_specs=in_specs + [
            pl.BlockSpec(memory_space=pl.ANY),
            pl.BlockSpec(memory_space=pl.ANY),
        ],
        out_specs=out_spec,
        out_shape=jax.ShapeDtypeStruct(x2.shape, F32),
        scratch_shapes=_pipelined_scratch(tm) + [
            pltpu.VMEM((D_MODEL, FFN_HIDDEN), F32),
            pltpu.VMEM((FFN_HIDDEN, D_MODEL), F32),
            pltpu.SemaphoreType.DMA((2, FFN_HIDDEN // FFN_HIDDEN_TILE)),
        ],
        compiler_params=_params(("arbitrary",)),
        name="ffn_relu2",
    )(x2, x2, mods, mods, norm_g, w1, w2)


def _gelu_x2(z):
    return z * (1.0 + lax.erf(z * (2.0 ** -0.5)))


SGU_OUT_K = 512


def _sgu_kernel(xn_ref, xp_ref, modn_ref, modp_ref, ng_ref, win_ref, bin_ref, lng_ref, lnb_ref,
                ws_ref, bsb_ref, wout_ref, o_ref, h_scr, y_scr, gp_scr, *, n):
    tm = xn_ref.shape[0]

    def core(slot, anchored, first):
        h = h_scr[slot]
        zv = _gelu_x2(_wdot(h, win_ref[:, G_HALF:]) + bin_ref[:, G_HALF:])
        mu = jnp.mean(zv, axis=-1, keepdims=True)
        zc = zv - mu
        var = jnp.mean(zc * zc, axis=-1, keepdims=True)
        vn = (zc * lax.rsqrt(var + 4.0 * EPS) * lng_ref[...] + lnb_ref[...]).astype(BF16)
        zu = _gelu_x2(_wdot(h, win_ref[:, :G_HALF]) + bin_ref[:, :G_HALF])

        row = lax.broadcasted_iota(jnp.int32, (G_CHUNK, G_CHUNK), 0)
        col = lax.broadcasted_iota(jnp.int32, (G_CHUNK, G_CHUNK), 1)
        causal = col <= row
        for g in range(G_GROUPS):
            c0 = g * G_GROUP
            wm = jnp.where(causal, 0.5 * ws_ref[g], 0.0).astype(BF16)
            bsg = bsb_ref[g]
            for r0 in range(0, tm, 2 * G_CHUNK):
                r1 = r0 + G_CHUNK
                r2 = r1 + G_CHUNK
                rhs = jnp.concatenate([vn[r0:r1, c0:c0 + G_GROUP],
                                       vn[r1:r2, c0:c0 + G_GROUP]], axis=1)
                sv = _dot(wm, rhs)
                gp_scr[r0:r1, c0:c0 + G_GROUP] = (
                    zu[r0:r1, c0:c0 + G_GROUP] * (sv[:, :G_GROUP] + bsg)).astype(BF16)
                gp_scr[r1:r2, c0:c0 + G_GROUP] = (
                    zu[r1:r2, c0:c0 + G_GROUP] * (sv[:, G_GROUP:] + bsg)).astype(BF16)
        for c0 in range(0, D_MODEL, MXU_COLS):
            y = None
            for k0 in range(0, G_HALF, SGU_OUT_K):
                yk = _wdot(gp_scr[:, k0:k0 + SGU_OUT_K],
                           anchored(wout_ref[k0:k0 + SGU_OUT_K, c0:c0 + MXU_COLS]))
                y = yk if y is None else y + yk
            y_scr[slot, :, c0:c0 + MXU_COLS] = y

    _pipelined(n, 64, (D_MODEL // MXU_COLS) * (G_HALF // SGU_OUT_K),
               (xn_ref, xp_ref, modn_ref, modp_ref, ng_ref, o_ref, h_scr, y_scr),
               _MIXER_ROWS, core, lambda: None)


def _sgu(x2, mods, norm_g, w_in, b_in, ln_g, ln_b, ws, bsb, w_out, layer, j, seq):
    tokens = x2.shape[0]
    tm = TOKEN_TILE
    per_seq = seq // tm
    n = tokens // tm
    in_specs, out_spec = _pipelined_specs(n, tm, per_seq, layer)
    return pl.pallas_call(
        functools.partial(_sgu_kernel, n=n),
        grid=(n + 1,),
        in_specs=in_specs + [
            _layer((D_MODEL, 2 * G_HALF), j),
            _layer((1, 2 * G_HALF), j),
            _layer((1, G_HALF), j),
            _layer((1, G_HALF), j),
            _layer((G_GROUPS, G_CHUNK, G_CHUNK), j),
            _layer((G_GROUPS, G_CHUNK, LANES), j),
            _layer((G_HALF, D_MODEL), j),
        ],
        out_specs=out_spec,
        out_shape=jax.ShapeDtypeStruct(x2.shape, F32),
        scratch_shapes=_pipelined_scratch(tm) + [pltpu.VMEM((tm, G_HALF), BF16)],
        compiler_params=_params(("arbitrary",)),
        name="sgu_mixer",
    )(x2, x2, mods, mods, norm_g, w_in, b_in, ln_g, ln_b, ws, bsb, w_out)


def _segment_scan_lanes(v, op, identity, seg):
    lane = lax.rem(lax.broadcasted_iota(jnp.int32, v.shape, v.ndim - 1), seg)
    s = 1
    while s < seg:
        v = op(v, jnp.where(lane >= s, pltpu.roll(v, s, axis=v.ndim - 1), identity))
        s *= 2
    return v


def _mlstm_kernel(x_ref, mod_ref, ng_ref, win_ref, wg_ref, bg_ref, hg_ref,
                  wout_ref, o_ref, p_scr, y_scr, s_scr, msub_scr, a_scr, kt_scr, hh_scr):
    L = M_CHUNK
    ts = x_ref.shape[0]

    @pl.when(pl.program_id(1) == 0)
    def _():
        s_scr[...] = jnp.zeros(s_scr.shape, F32)
        msub_scr[...] = jnp.zeros(msub_scr.shape, F32)

    row = lax.broadcasted_iota(jnp.int32, (L, L), 0)
    col = lax.broadcasted_iota(jnp.int32, (L, L), 1)
    causal = col <= row
    ones_blk = jnp.ones((L, M_DV), BF16)
    zeros_k = jnp.zeros((M_DK, L), F32)
    zeros_s = jnp.zeros((M_DK, 2 * M_DV), F32)

    def chunk(c):
        rows = slice(c * L, (c + 1) * L)

        bc_row = bc_all[:, rows]
        r_row = r_all[:, rows]
        cmax = cmax_all[:, rows]
        m_st = msub_scr[...]
        mx = jnp.maximum(cmax, m_st)
        rmax = cmax[:, L - 1:L]
        wgt_row = jnp.exp(r_row - rmax)
        b_last = bc_row[:, L - 1:L]
        m_loc = b_last + rmax
        m_new = jnp.maximum(b_last + m_st, m_loc)
        s_old = jnp.exp(b_last + m_st - m_new)
        s_new = jnp.exp(m_loc - m_new)
        msub_scr[...] = m_new

        r2_row = LOG2_E * r_row
        per_row = jnp.concatenate(
            [jnp.exp2(LOG2_E * (m_st - mx)), jnp.exp2(-LOG2_E * (bc_row + mx)), LOG2_E * mx,
             jnp.zeros((LANES - 3 * M_HEADS, L), F32)], axis=0).T

        for pair in range(M_HEADS // 2):
            pc = pair * LANES
            q2f = p_scr[rows, pc:pc + LANES]
            kT = (p_scr[rows, M_QK + pc:M_QK + pc + LANES] * (M_DK ** -0.5)).T
            kt_scr[c, pair] = kT
            bd = jnp.concatenate(
                [jnp.concatenate([kT[0:M_DK], zeros_k], axis=0),
                 jnp.concatenate([zeros_k, kT[M_DK:2 * M_DK]], axis=0)], axis=1)
            s_pair = _dot(q2f.astype(BF16), bd.astype(BF16))
            for half in range(2):
                hd = 2 * pair + half
                m_rep = jnp.broadcast_to(per_row[:, 2 * M_HEADS + hd:2 * M_HEADS + hd + 1],
                                         (L, LANES))
                decay = jnp.where(causal, jnp.exp2(r2_row[hd:hd + 1, :] - m_rep), 0.0)
                scores = (s_pair[:, half * L:(half + 1) * L] * decay).astype(BF16)
                qi = (q2f * jnp.broadcast_to(per_row[:, hd:hd + 1], (L, LANES))).astype(BF16)
                a_scr[c, hd] = jnp.concatenate([scores, qi], axis=1)

        for hd in range(M_HEADS):
            pair, half = divmod(hd, 2)
            vc = 2 * M_QK + hd * M_DV
            vext = jnp.concatenate([p_scr[rows, vc:vc + M_DV].astype(BF16), ones_blk], axis=1)
            s_h = s_scr[pair, half * M_DK:(half + 1) * M_DK, :]
            sz = jnp.concatenate([s_h, zeros_s] if half == 0 else [zeros_s, s_h], axis=0)
            nd = _dot(a_scr[c, hd], jnp.concatenate([vext, sz.astype(BF16)], axis=0))
            floor = jnp.broadcast_to(per_row[:, M_HEADS + hd:M_HEADS + hd + 1], (L, LANES))
            hh_scr[c, :, hd * M_DV:(hd + 1) * M_DV] = nd[:, :M_DV] / jnp.maximum(
                jnp.abs(nd[:, M_DV:]), floor)

        for hd in range(M_HEADS):
            hc = hd * M_DV
            oc = 2 * M_QK + D_MODEL + hc
            hh = hh_scr[c, :, hc:hc + M_DV]
            hh = hh * lax.rsqrt(jnp.mean(hh * hh, axis=-1, keepdims=True) + EPS)
            hh = hh * hg_ref[:, hc:hc + M_DV]
            y_scr[rows, hc:hc + M_DV] = (jax.nn.sigmoid(p_scr[rows, oc:oc + M_DV]) * hh).astype(BF16)

        for hd in range(M_HEADS):
            pair, half = divmod(hd, 2)
            vc = 2 * M_QK + hd * M_DV
            vext = jnp.concatenate([p_scr[rows, vc:vc + M_DV].astype(BF16), ones_blk], axis=1)
            kT_h = kt_scr[c, pair, half * M_DK:(half + 1) * M_DK, :]
            upd = _dot((kT_h * wgt_row[hd:hd + 1, :]).astype(BF16), vext)
            s_h = s_scr[pair, half * M_DK:(half + 1) * M_DK, :]
            s_scr[pair, half * M_DK:(half + 1) * M_DK, :] = (
                jnp.concatenate([s_old[hd:hd + 1, :], s_old[hd:hd + 1, :]], axis=1) * s_h
                + jnp.concatenate([s_new[hd:hd + 1, :], s_new[hd:hd + 1, :]], axis=1) * upd)

    x = x_ref[...]
    mod = mod_ref[...]
    ng = ng_ref[...]
    h = (_rms(x, ng[0:1] * (1.0 + mod[1:2])) + mod[0:1]).astype(BF16)
    g_t = (_wdot(h, wg_ref[...]) + bg_ref[...]).T
    ii_all = GATE_CAP * jnp.tanh(g_t[0:M_HEADS] / GATE_CAP)
    logf = jax.nn.log_sigmoid(GATE_CAP * jnp.tanh(g_t[M_HEADS:2 * M_HEADS] / GATE_CAP))
    bc_all = _segment_scan_lanes(logf, jnp.add, 0.0, L)
    r_all = ii_all - bc_all
    cmax_all = _segment_scan_lanes(r_all, jnp.maximum, -jnp.inf, L)
    last = M_QKVO - MXU_COLS
    p_scr[:, :last] = _wdot(h, win_ref[:, :last])
    scanned = _zero_words(cmax_all)
    scanned = scanned[:, :MXU_COLS] | scanned[:, MXU_COLS:]
    p_scr[:, last:] = _wdot(h, _after(win_ref[:, last:], scanned))
    for c in range(ts // L):
        chunk(c)
    y = _wdot(y_scr[...], wout_ref[...])
    o_ref[...] = x + _rms(y, mod[2:3] * ng[1:2])


def _pad_lanes(a):
    return jnp.pad(a, [(0, 0)] * (a.ndim - 1) + [(0, LANES - a.shape[-1])])


def _mlstm(x2, mods, norm_g, w_in, w_g, b_g, hnorm_g, w_out, layer, j, bsz, seq):
    ts = TOKEN_TILE
    per_seq = seq // ts
    chunks = ts // M_CHUNK
    return pl.pallas_call(
        _mlstm_kernel,
        grid=(bsz, per_seq),
        in_specs=[
            pl.BlockSpec((ts, D_MODEL), lambda b, t: (b * per_seq + t, 0)),
            pl.BlockSpec((None, None, 6, D_MODEL), lambda b, t: (layer, b, 0, 0)),
            _layer((4, D_MODEL), layer),
            _layer((D_MODEL, M_QKVO), j),
            _layer((D_MODEL, LANES), j),
            _layer((1, LANES), j),
            _layer((1, D_MODEL), j),
            _layer((D_MODEL, D_MODEL), j),
        ],
        out_specs=pl.BlockSpec((ts, D_MODEL), lambda b, t: (b * per_seq + t, 0)),
        out_shape=jax.ShapeDtypeStruct(x2.shape, F32),
        scratch_shapes=[
            pltpu.VMEM((ts, M_QKVO), F32),
            pltpu.VMEM((ts, D_MODEL), BF16),
            pltpu.VMEM((M_HEADS // 2, 2 * M_DK, 2 * M_DV), F32),
            pltpu.VMEM((M_HEADS, LANES), F32),
            pltpu.VMEM((chunks, M_HEADS, M_CHUNK, 2 * LANES), BF16),
            pltpu.VMEM((chunks, M_HEADS // 2, LANES, M_CHUNK), F32),
            pltpu.VMEM((chunks, M_CHUNK, D_MODEL), F32),
        ],
        compiler_params=_params(("arbitrary", "arbitrary")),
        name="mlstm_mixer",
    )(x2, mods, norm_g, w_in, w_g, b_g, hnorm_g, w_out)


def kernel(x, c, norm_g, ada_w, ada_b, ffn_w1, ffn_w2, a_w_in, a_b_if, a_hnorm_g, a_w_out,
           b_w_in, b_b_in, b_ln_g, b_ln_b, b_ws, b_bs, b_w_out):
    bsz, seq, d = x.shape
    assert d == D_MODEL and seq % TOKEN_TILE == 0 and norm_g.shape[0] == DEPTH
    mods = _ada(c, ada_w, ada_b).reshape(DEPTH, bsz, 6, D_MODEL)
    a_w_g = _pad_lanes(a_w_in[..., M_QKVO:])
    a_b_g = _pad_lanes(a_b_if)[:, None, :]
    a_hnorm_g = a_hnorm_g[:, None, :]
    b_b_in, b_ln_g, b_ln_b = b_b_in[:, None, :], b_ln_g[:, None, :], b_ln_b[:, None, :]
    b_bsb = jnp.broadcast_to(0.5 * b_bs[..., None], b_bs.shape + (LANES,))
    x2 = x.reshape(bsz * seq, D_MODEL)
    for i in range(DEPTH):
        j = i // 2
        if i % 2 == 0:
            x2 = _mlstm(x2, mods, norm_g, a_w_in, a_w_g, a_b_g, a_hnorm_g, a_w_out,
                        i, j, bsz, seq)
        else:
            x2 = _sgu(x2, mods, norm_g, b_w_in, b_b_in, b_ln_g, b_ln_b, b_ws, b_bsb, b_w_out,
                      i, j, seq)
        x2 = _ffn(x2, mods, norm_g, ffn_w1, ffn_w2, i, seq)
    return x2.reshape(bsz, seq, D_MODEL)
```

```python
import functools

import jax
import jax.numpy as jnp
from jax import lax
from jax.experimental import pallas as pl
from jax.experimental.pallas import tpu as pltpu

F32 = jnp.float32
BF16 = jnp.bfloat16

D_MODEL = 1024
DEPTH = 4
EPS = 1e-6

M_HEADS = 8
M_DV = 128
M_DK = 64
M_CHUNK = 128
GATE_CAP = 15.0
LOG2_E = 1.4426950408889634
M_QK = M_HEADS * M_DK
M_QKVO = 2 * M_QK + 2 * D_MODEL

G_CHUNK = 128
G_HALF = 2048
G_GROUP = 128
G_GROUPS = 16

FFN_HIDDEN = 4096

LANES = 128
MXU_COLS = 256
VMEM_LIMIT_BYTES = 60 * 1024 * 1024

TOKEN_TILE = 512
FFN_HIDDEN_TILE = 1024


def _dot(a, b):
    return jnp.dot(a, b, preferred_element_type=F32)


def _wdot(a, w):
    return _dot(a, w.astype(BF16))


def _rms(x, g):
    return x * lax.rsqrt(jnp.mean(x * x, axis=-1, keepdims=True) + EPS) * g


def _zero_words(v):
    u = pltpu.bitcast(v, jnp.uint32)
    acc = u[0:8]
    for r in range(8, u.shape[0], 8):
        acc = acc | u[r:r + 8]
    return (acc >> 16) >> 16


def _after(w, zero_words):
    rows = 8 * 4 // w.dtype.itemsize
    k = zero_words.shape[1]
    top = pltpu.bitcast(pltpu.bitcast(w[0:rows, 0:k], jnp.uint32) | zero_words, w.dtype)
    if k < w.shape[1]:
        top = jnp.concatenate([top, w[0:rows, k:]], axis=1)
    return jnp.concatenate([top, w[rows:]], axis=0)


def _layer(shape, layer):
    index = (layer,) + (0,) * len(shape)
    return pl.BlockSpec((None,) + tuple(shape), lambda *_: index, pipeline_mode=pl.Buffered(1))


def _params(semantics, flags=None):
    return pltpu.CompilerParams(dimension_semantics=semantics,
                                vmem_limit_bytes=VMEM_LIMIT_BYTES, flags=flags)


def _ada_kernel(c_ref, w_ref, b_ref, o_ref):
    c = c_ref[...]
    ca = (c * jax.nn.sigmoid(c)).astype(BF16)
    o_ref[...] = _wdot(ca, w_ref[...]) + b_ref[...]


def _ada(c, ada_w, ada_b):
    bsz = c.shape[0]
    blk = 2 * D_MODEL
    return pl.pallas_call(
        _ada_kernel,
        grid=(DEPTH, 6 * D_MODEL // blk),
        in_specs=[
            pl.BlockSpec((bsz, D_MODEL), lambda i, j: (0, 0)),
            pl.BlockSpec((None, D_MODEL, blk), lambda i, j: (i, 0, j)),
            pl.BlockSpec((None, 1, blk), lambda i, j: (i, 0, j)),
        ],
        out_specs=pl.BlockSpec((None, bsz, blk), lambda i, j: (i, 0, j)),
        out_shape=jax.ShapeDtypeStruct((DEPTH, bsz, 6 * D_MODEL), F32),
        compiler_params=_params(("arbitrary", "arbitrary")),
        name="ada_mod",
    )(c, ada_w, ada_b.reshape(DEPTH, 1, 6 * D_MODEL))


_MIXER_ROWS = (0, 0, 1, 1, 2)
_FFN_ROWS = (2, 3, 4, 3, 5)


def _pipelined(n, piece, tiles, refs, rows, core, start_fetch):
    xn_ref, xp_ref, modn_ref, modp_ref, ng_ref, o_ref, h_scr, y_scr = refs
    g_in, shift, scale, g_out, gate = rows
    i = pl.program_id(0)
    tm = xn_ref.shape[0]
    ng = ng_ref[...]

    def normalise(x_ref, mod_ref, slot, r0):
        mod = mod_ref[...]
        gain = ng[g_in:g_in + 1] * (1.0 + mod[scale:scale + 1])
        hn = (_rms(x_ref[r0:r0 + piece, :], gain) + mod[shift:shift + 1]).astype(BF16)
        h_scr[slot, r0:r0 + piece, :] = hn
        return hn

    def finish(slot, r0):
        gain = modp_ref[gate:gate + 1, :] * ng[g_out:g_out + 1]
        o = xp_ref[r0:r0 + piece, :] + _rms(y_scr[slot, r0:r0 + piece, :], gain)
        o_ref[r0:r0 + piece, :] = o
        return o

    @pl.when(i == 0)
    def _():
        start_fetch()
        for r0 in range(0, tm, piece):
            normalise(xp_ref, modp_ref, 0, r0)
        y_scr[1] = jnp.zeros(y_scr.shape[1:], F32)

    def step(slot, first):
        work = []
        for r0 in range(0, tm, piece):
            work.append(functools.partial(finish, 1 - slot, r0))
            work.append(functools.partial(normalise, xn_ref, modn_ref, 1 - slot, r0))

        left = [tiles]

        def anchored(w):
            take = -(-len(work) // left[0])
            left[0] -= 1
            for _ in range(take):
                w = _after(w, _zero_words(work.pop(0)())[:, :w.shape[1]])
            return w

        core(slot, anchored, first)
        assert not work and left[0] == 0, "core must pass exactly `tiles` weight tiles"

    even = lax.rem(i, 2) == 0
    pl.when(i == 0)(lambda: step(0, True))
    pl.when((i > 0) & (i < n) & even)(lambda: step(0, False))
    pl.when((i < n) & jnp.logical_not(even))(lambda: step(1, False))

    @pl.when(i == n)
    def _():
        for r0 in range(0, tm, piece):
            finish((n - 1) % 2, r0)


def _pipelined_specs(n, tm, per_seq, layer):
    nxt = lambda i: jnp.minimum(i + 1, n - 1)
    prv = lambda i: jnp.maximum(i - 1, 0)
    in_specs = [
        pl.BlockSpec((tm, D_MODEL), lambda i: (nxt(i), 0)),
        pl.BlockSpec((tm, D_MODEL), lambda i: (prv(i), 0)),
        pl.BlockSpec((None, None, 6, D_MODEL), lambda i: (layer, nxt(i) // per_seq, 0, 0)),
        pl.BlockSpec((None, None, 6, D_MODEL), lambda i: (layer, prv(i) // per_seq, 0, 0)),
        _layer((4, D_MODEL), layer),
    ]
    return in_specs, pl.BlockSpec((tm, D_MODEL), lambda i: (prv(i), 0))


def _pipelined_scratch(tm):
    return [pltpu.VMEM((2, tm, D_MODEL), BF16),
            pltpu.VMEM((2, tm, D_MODEL), F32)]


def _ffn_kernel(xn_ref, xp_ref, modn_ref, modp_ref, ng_ref, w1_hbm, w2_hbm, o_ref,
                h_scr, y_scr, w1_ref, w2_ref, sem, *, n, layer):
    chunks = list(range(0, FFN_HIDDEN, FFN_HIDDEN_TILE))

    def fetch(which, k):
        k0 = chunks[k]
        if which == 0:
            src = w1_hbm.at[layer, :, k0:k0 + FFN_HIDDEN_TILE]
            dst = w1_ref.at[:, k0:k0 + FFN_HIDDEN_TILE]
        else:
            src = w2_hbm.at[layer, k0:k0 + FFN_HIDDEN_TILE, :]
            dst = w2_ref.at[k0:k0 + FFN_HIDDEN_TILE, :]
        return pltpu.make_async_copy(src, dst, sem.at[which, k])

    def start_fetch():
        for k in range(len(chunks)):
            fetch(0, k).start()
            fetch(1, k).start()

    def core(slot, anchored, first):
        h = h_scr[slot]
        y = [None] * (D_MODEL // MXU_COLS)
        for k, k0 in enumerate(chunks):
            if first:
                fetch(0, k).wait()
            a = []
            for c0 in range(k0, k0 + FFN_HIDDEN_TILE, MXU_COLS):
                aj = _wdot(h, anchored(w1_ref[:, c0:c0 + MXU_COLS]))
                a.append(jnp.square(jnp.maximum(aj, 0.0)).astype(BF16))
            a = jnp.concatenate(a, axis=1)
            if first:
                fetch(1, k).wait()
            for j, c0 in enumerate(range(0, D_MODEL, MXU_COLS)):
                yk = _wdot(a, anchored(w2_ref[k0:k0 + FFN_HIDDEN_TILE, c0:c0 + MXU_COLS]))
                y[j] = yk if y[j] is None else y[j] + yk
        y_scr[slot] = jnp.concatenate(y, axis=1)

    tiles = (FFN_HIDDEN // MXU_COLS) + (FFN_HIDDEN // FFN_HIDDEN_TILE) * (D_MODEL // MXU_COLS)
    _pipelined(n, 32, tiles, (xn_ref, xp_ref, modn_ref, modp_ref, ng_ref, o_ref, h_scr, y_scr),
               _FFN_ROWS, core, start_fetch)


def _ffn(x2, mods, norm_g, w1, w2, layer, seq):
    tokens = x2.shape[0]
    tm = TOKEN_TILE
    per_seq = seq // tm
    n = tokens // tm
    in_specs, out_spec = _pipelined_specs(n, tm, per_seq, layer)
    return pl.pallas_call(
        functools.partial(_ffn_kernel, n=n, layer=layer),
        grid=(n + 1,),
        in_specs=in_specs + [
            pl.BlockSpec(memory_space=pl.ANY),
            pl.BlockSpec(memory_space=pl.ANY),
        ],
        out_specs=out_spec,
        out_shape=jax.ShapeDtypeStruct(x2.shape, F32),
        scratch_shapes=_pipelined_scratch(tm) + [
            pltpu.VMEM((D_MODEL, FFN_HIDDEN), F32),
            pltpu.VMEM((FFN_HIDDEN, D_MODEL), F32),
            pltpu.SemaphoreType.DMA((2, FFN_HIDDEN // FFN_HIDDEN_TILE)),
        ],
        compiler_params=_params(("arbitrary",)),
        name="ffn_relu2",
    )(x2, x2, mods, mods, norm_g, w1, w2)


def _gelu_x2(z):
    return z * (1.0 + lax.erf(z * (2.0 ** -0.5)))


def _sgu_kernel(x_ref, mod_ref, ng_ref, win_ref, bin_ref, lng_ref, lnb_ref,
                ws_ref, bsb_ref, wout_ref, o_ref, y_scr):
    x = x_ref[...]
    mod = mod_ref[...]
    ng = ng_ref[...]
    tm = x.shape[0]
    h = (_rms(x, ng[0:1] * (1.0 + mod[1:2])) + mod[0:1]).astype(BF16)
    zv = _gelu_x2(_wdot(h, win_ref[:, G_HALF:]) + bin_ref[:, G_HALF:])
    zu0 = _gelu_x2(_wdot(h, win_ref[:, :MXU_COLS]) + bin_ref[:, :MXU_COLS])
    mu = jnp.mean(zv, axis=-1, keepdims=True)
    mu = pltpu.bitcast(pltpu.bitcast(mu, jnp.uint32) | _zero_words(zu0)[0:1, 0:1], F32)
    zc = zv - mu
    var = jnp.mean(zc * zc, axis=-1, keepdims=True)
    vn = (zc * lax.rsqrt(var + 4.0 * EPS) * lng_ref[...] + lnb_ref[...]).astype(BF16)
    zu = jnp.concatenate(
        [zu0, _gelu_x2(_wdot(h, win_ref[:, MXU_COLS:G_HALF]) + bin_ref[:, MXU_COLS:G_HALF])],
        axis=1)

    row = lax.broadcasted_iota(jnp.int32, (G_CHUNK, G_CHUNK), 0)
    col = lax.broadcasted_iota(jnp.int32, (G_CHUNK, G_CHUNK), 1)
    causal = col <= row
    for g in range(G_GROUPS):
        c0 = g * G_GROUP
        wm = jnp.where(causal, 0.5 * ws_ref[g], 0.0).astype(BF16)
        bsg = bsb_ref[g]
        for r0 in range(0, tm, 2 * G_CHUNK):
            r1 = r0 + G_CHUNK
            r2 = r1 + G_CHUNK
            rhs = jnp.concatenate([vn[r0:r1, c0:c0 + G_GROUP],
                                   vn[r1:r2, c0:c0 + G_GROUP]], axis=1)
            sv = _dot(wm, rhs)
            y_scr[r0:r1, c0:c0 + G_GROUP] = (
                zu[r0:r1, c0:c0 + G_GROUP] * (sv[:, :G_GROUP] + bsg)).astype(BF16)
            y_scr[r1:r2, c0:c0 + G_GROUP] = (
                zu[r1:r2, c0:c0 + G_GROUP] * (sv[:, G_GROUP:] + bsg)).astype(BF16)
    y = _wdot(y_scr[...], wout_ref[...])
    o_ref[...] = x + _rms(y, mod[2:3] * ng[1:2])


def _sgu(x2, mods, norm_g, w_in, b_in, ln_g, ln_b, ws, bsb, w_out, layer, j, seq):
    tokens = x2.shape[0]
    tm = TOKEN_TILE
    per_seq = seq // tm
    return pl.pallas_call(
        _sgu_kernel,
        grid=(tokens // tm,),
        in_specs=[
            pl.BlockSpec((tm, D_MODEL), lambda i: (i, 0)),
            pl.BlockSpec((None, None, 6, D_MODEL), lambda i: (layer, i // per_seq, 0, 0)),
            _layer((4, D_MODEL), layer),
            _layer((D_MODEL, 2 * G_HALF), j),
            _layer((1, 2 * G_HALF), j),
            _layer((1, G_HALF), j),
            _layer((1, G_HALF), j),
            _layer((G_GROUPS, G_CHUNK, G_CHUNK), j),
            _layer((G_GROUPS, G_CHUNK, LANES), j),
            _layer((G_HALF, D_MODEL), j),
        ],
        out_specs=pl.BlockSpec((tm, D_MODEL), lambda i: (i, 0)),
        out_shape=jax.ShapeDtypeStruct(x2.shape, F32),
        scratch_shapes=[pltpu.VMEM((tm, G_HALF), BF16)],
        compiler_params=_params(("arbitrary",)),
        name="sgu_mixer",
    )(x2, mods, norm_g, w_in, b_in, ln_g, ln_b, ws, bsb, w_out)


def _segment_scan_lanes(v, op, identity, seg):
    lane = lax.rem(lax.broadcasted_iota(jnp.int32, v.shape, v.ndim - 1), seg)
    s = 1
    while s < seg:
        v = op(v, jnp.where(lane >= s, pltpu.roll(v, s, axis=v.ndim - 1), identity))
        s *= 2
    return v


def _mlstm_kernel(x_ref, mod_ref, ng_ref, win_ref, wg_ref, bg_ref, hg_ref,
                  wout_ref, o_ref, p_scr, y_scr, s_scr, msub_scr, a_scr, kt_scr, hh_scr):
    L = M_CHUNK
    ts = x_ref.shape[0]

    @pl.when(pl.program_id(1) == 0)
    def _():
        s_scr[...] = jnp.zeros(s_scr.shape, F32)
        msub_scr[...] = jnp.zeros(msub_scr.shape, F32)

    row = lax.broadcasted_iota(jnp.int32, (L, L), 0)
    col = lax.broadcasted_iota(jnp.int32, (L, L), 1)
    causal = col <= row
    ones_blk = jnp.ones((L, M_DV), BF16)
    zeros_k = jnp.zeros((M_DK, L), F32)
    zeros_s = jnp.zeros((M_DK, 2 * M_DV), F32)

    def chunk(c):
        rows = slice(c * L, (c + 1) * L)

        bc_row = bc_all[:, rows]
        r_row = r_all[:, rows]
        cmax = cmax_all[:, rows]
        m_st = msub_scr[...]
        mx = jnp.maximum(cmax, m_st)
        rmax = cmax[:, L - 1:L]
        wgt_row = jnp.exp(r_row - rmax)
        b_last = bc_row[:, L - 1:L]
        m_loc = b_last + rmax
        m_new = jnp.maximum(b_last + m_st, m_loc)
        s_old = jnp.exp(b_last + m_st - m_new)
        s_new = jnp.exp(m_loc - m_new)
        msub_scr[...] = m_new

        r2_row = LOG2_E * r_row
        per_row = jnp.concatenate(
            [jnp.exp2(LOG2_E * (m_st - mx)), jnp.exp2(-LOG2_E * (bc_row + mx)), LOG2_E * mx,
             jnp.zeros((LANES - 3 * M_HEADS, L), F32)], axis=0).T

        for pair in range(M_HEADS // 2):
            pc = pair * LANES
            q2f = p_scr[rows, pc:pc + LANES]
            kT = (p_scr[rows, M_QK + pc:M_QK + pc + LANES] * (M_DK ** -0.5)).T
            kt_scr[c, pair] = kT
            bd = jnp.concatenate(
                [jnp.concatenate([kT[0:M_DK], zeros_k], axis=0),
                 jnp.concatenate([zeros_k, kT[M_DK:2 * M_DK]], axis=0)], axis=1)
            s_pair = _dot(q2f.astype(BF16), bd.astype(BF16))
            for half in range(2):
                hd = 2 * pair + half
                m_rep = jnp.broadcast_to(per_row[:, 2 * M_HEADS + hd:2 * M_HEADS + hd + 1],
                                         (L, LANES))
                decay = jnp.where(causal, jnp.exp2(r2_row[hd:hd + 1, :] - m_rep), 0.0)
                scores = (s_pair[:, half * L:(half + 1) * L] * decay).astype(BF16)
                qi = (q2f * jnp.broadcast_to(per_row[:, hd:hd + 1], (L, LANES))).astype(BF16)
                a_scr[c, hd] = jnp.concatenate([scores, qi], axis=1)

        for hd in range(M_HEADS):
            pair, half = divmod(hd, 2)
            vc = 2 * M_QK + hd * M_DV
            vext = jnp.concatenate([p_scr[rows, vc:vc + M_DV].astype(BF16), ones_blk], axis=1)
            s_h = s_scr[pair, half * M_DK:(half + 1) * M_DK, :]
            sz = jnp.concatenate([s_h, zeros_s] if half == 0 else [zeros_s, s_h], axis=0)
            nd = _dot(a_scr[c, hd], jnp.concatenate([vext, sz.astype(BF16)], axis=0))
            floor = jnp.broadcast_to(per_row[:, M_HEADS + hd:M_HEADS + hd + 1], (L, LANES))
            hh_scr[c, :, hd * M_DV:(hd + 1) * M_DV] = nd[:, :M_DV] / jnp.maximum(
                jnp.abs(nd[:, M_DV:]), floor)

        for hd in range(M_HEADS):
            hc = hd * M_DV
            oc = 2 * M_QK + D_MODEL + hc
            hh = hh_scr[c, :, hc:hc + M_DV]
            hh = hh * lax.rsqrt(jnp.mean(hh * hh, axis=-1, keepdims=True) + EPS)
            hh = hh * hg_ref[:, hc:hc + M_DV]
            y_scr[rows, hc:hc + M_DV] = (jax.nn.sigmoid(p_scr[rows, oc:oc + M_DV]) * hh).astype(BF16)

        for hd in range(M_HEADS):
            pair, half = divmod(hd, 2)
            vc = 2 * M_QK + hd * M_DV
            vext = jnp.concatenate([p_scr[rows, vc:vc + M_DV].astype(BF16), ones_blk], axis=1)
            kT_h = kt_scr[c, pair, half * M_DK:(half + 1) * M_DK, :]
            upd = _dot((kT_h * wgt_row[hd:hd + 1, :]).astype(BF16), vext)
            s_h = s_scr[pair, half * M_DK:(half + 1) * M_DK, :]
            s_scr[pair, half * M_DK:(half + 1) * M_DK, :] = (
                jnp.concatenate([s_old[hd:hd + 1, :], s_old[hd:hd + 1, :]], axis=1) * s_h
                + jnp.concatenate([s_new[hd:hd + 1, :], s_new[hd:hd + 1, :]], axis=1) * upd)

    x = x_ref[...]
    mod = mod_ref[...]
    ng = ng_ref[...]
    h = (_rms(x, ng[0:1] * (1.0 + mod[1:2])) + mod[0:1]).astype(BF16)
    g_t = (_wdot(h, wg_ref[...]) + bg_ref[...]).T
    ii_all = GATE_CAP * jnp.tanh(g_t[0:M_HEADS] / GATE_CAP)
    logf = jax.nn.log_sigmoid(GATE_CAP * jnp.tanh(g_t[M_HEADS:2 * M_HEADS] / GATE_CAP))
    bc_all = _segment_scan_lanes(logf, jnp.add, 0.0, L)
    r_all = ii_all - bc_all
    cmax_all = _segment_scan_lanes(r_all, jnp.maximum, -jnp.inf, L)
    last = M_QKVO - MXU_COLS
    p_scr[:, :last] = _wdot(h, win_ref[:, :last])
    scanned = _zero_words(cmax_all)
    scanned = scanned[:, :MXU_COLS] | scanned[:, MXU_COLS:]
    p_scr[:, last:] = _wdot(h, _after(win_ref[:, last:], scanned))
    for c in range(ts // L):
        chunk(c)
    y = _wdot(y_scr[...], wout_ref[...])
    o_ref[...] = x + _rms(y, mod[2:3] * ng[1:2])


def _pad_lanes(a):
    return jnp.pad(a, [(0, 0)] * (a.ndim - 1) + [(0, LANES - a.shape[-1])])


def _mlstm(x2, mods, norm_g, w_in, w_g, b_g, hnorm_g, w_out, layer, j, bsz, seq):
    ts = TOKEN_TILE
    per_seq = seq // ts
    chunks = ts // M_CHUNK
    return pl.pallas_call(
        _mlstm_kernel,
        grid=(bsz, per_seq),
        in_specs=[
            pl.BlockSpec((ts, D_MODEL), lambda b, t: (b * per_seq + t, 0)),
            pl.BlockSpec((None, None, 6, D_MODEL), lambda b, t: (layer, b, 0, 0)),
            _layer((4, D_MODEL), layer),
            _layer((D_MODEL, M_QKVO), j),
            _layer((D_MODEL, LANES), j),
            _layer((1, LANES), j),
            _layer((1, D_MODEL), j),
            _layer((D_MODEL, D_MODEL), j),
        ],
        out_specs=pl.BlockSpec((ts, D_MODEL), lambda b, t: (b * per_seq + t, 0)),
        out_shape=jax.ShapeDtypeStruct(x2.shape, F32),
        scratch_shapes=[
            pltpu.VMEM((ts, M_QKVO), F32),
            pltpu.VMEM((ts, D_MODEL), BF16),
            pltpu.VMEM((M_HEADS // 2, 2 * M_DK, 2 * M_DV), F32),
            pltpu.VMEM((M_HEADS, LANES), F32),
            pltpu.VMEM((chunks, M_HEADS, M_CHUNK, 2 * LANES), BF16),
            pltpu.VMEM((chunks, M_HEADS // 2, LANES, M_CHUNK), F32),
            pltpu.VMEM((chunks, M_CHUNK, D_MODEL), F32),
        ],
        compiler_params=_params(("arbitrary", "arbitrary")),
        name="mlstm_mixer",
    )(x2, mods, norm_g, w_in, w_g, b_g, hnorm_g, w_out)


def kernel(x, c, norm_g, ada_w, ada_b, ffn_w1, ffn_w2, a_w_in, a_b_if, a_hnorm_g, a_w_out,
           b_w_in, b_b_in, b_ln_g, b_ln_b, b_ws, b_bs, b_w_out):
    bsz, seq, d = x.shape
    assert d == D_MODEL and seq % TOKEN_TILE == 0 and norm_g.shape[0] == DEPTH
    mods = _ada(c, ada_w, ada_b).reshape(DEPTH, bsz, 6, D_MODEL)
    a_w_g = _pad_lanes(a_w_in[..., M_QKVO:])
    a_b_g = _pad_lanes(a_b_if)[:, None, :]
    a_hnorm_g = a_hnorm_g[:, None, :]
    b_b_in, b_ln_g, b_ln_b = b_b_in[:, None, :], b_ln_g[:, None, :], b_ln_b[:, None, :]
    b_bsb = jnp.broadcast_to(0.5 * b_bs[..., None], b_bs.shape + (LANES,))
    x2 = x.reshape(bsz * seq, D_MODEL)
    for i in range(DEPTH):
        j = i // 2
        if i % 2 == 0:
            x2 = _mlstm(x2, mods, norm_g, a_w_in, a_w_g, a_b_g, a_hnorm_g, a_w_out,
                        i, j, bsz, seq)
        else:
            x2 = _sgu(x2, mods, norm_g, b_w_in, b_b_in, b_ln_g, b_ln_b, b_ws, b_bsb, b_w_out,
                      i, j, seq)
        x2 = _ffn(x2, mods, norm_g, ffn_w1, ffn_w2, i, seq)
    return x2.reshape(bsz, seq, D_MODEL)
```
